```python
import math
import jax, jax.numpy as jnp
from jax import lax
import numpy as np

D_MODEL = 1024
BATCH = 4
SEQ = 4096
DEPTH = 1
DEC_BATCH = 32
DEC_SEQ = 4
PAST_LEN = 16384
PAGE_SIZE = 128

N_HEADS = 8
HEAD_DIM = D_MODEL // N_HEADS
ATTN_WIDTH = N_HEADS * HEAD_DIM
MOBA_BLOCK = 256
MOBA_TOP_K = 3
Q_CHUNK = 32
POOL_WINDOWS = (2, 4, 8, 16)
N_POOL_GROUPS = 4
POOL_GROUP_DIM = D_MODEL // 8
POOL_WIDTH = N_POOL_GROUPS * POOL_GROUP_DIM
POOL_OUT_DIM = D_MODEL // N_POOL_GROUPS
POOL_BUF = max(POOL_WINDOWS) - 1
D_FF = 4 * D_MODEL
IN_WIDTH = 3 * ATTN_WIDTH + POOL_WIDTH + 2 * D_MODEL
EPS = 1e-6
NEG_INF = -1e30

kernel_name = "gated_pool_moba_decoder_step"


def rmsnorm(x, g):
    xf = x.astype(jnp.float32)
    y = xf * lax.rsqrt(jnp.mean(xf * xf, axis=-1, keepdims=True) + EPS) * g.astype(jnp.float32)
    return y.astype(x.dtype)


def project_in(x, norm1_g, w_in, q_norm_g, k_norm_g):
    b, t, _ = x.shape
    z = rmsnorm(x, norm1_g) @ w_in
    o1 = ATTN_WIDTH
    o2 = 2 * ATTN_WIDTH
    o3 = 3 * ATTN_WIDTH
    o4 = o3 + POOL_WIDTH
    o5 = o4 + D_MODEL
    q = rmsnorm(z[..., :o1].reshape(b, t, N_HEADS, HEAD_DIM), q_norm_g)
    k = rmsnorm(z[..., o1:o2].reshape(b, t, N_HEADS, HEAD_DIM), k_norm_g)
    v = z[..., o2:o3].reshape(b, t, N_HEADS, HEAD_DIM)
    u = z[..., o3:o4]
    g_pool = z[..., o4:o5]
    g_attn = z[..., o5:]
    return q, k, v, u, g_pool, g_attn


def multiscale_pool(u_ext, n_prev, pos0):
    b, l, _ = u_ext.shape
    t = l - n_prev
    uf = u_ext.astype(jnp.float32).reshape(b, l, N_POOL_GROUPS, POOL_GROUP_DIM)
    cs = jnp.concatenate([jnp.zeros((b, 1, N_POOL_GROUPS, POOL_GROUP_DIM), jnp.float32),
                          jnp.cumsum(uf, axis=1)], axis=1)
    rows = n_prev + jnp.arange(t)
    w = jnp.array(POOL_WINDOWS, jnp.int32)
    cnt = jnp.minimum(w[None, :], pos0 + rows[:, None] + 1)
    lo = rows[:, None] + 1 - cnt
    g_idx = jnp.arange(N_POOL_GROUPS)[None, :]
    s_hi = cs[:, n_prev + 1:]
    s_lo = cs[:, lo, g_idx]
    mean = (s_hi - s_lo) / cnt[None, :, :, None].astype(jnp.float32)
    return mean - uf[:, n_prev:]


def pool_branch(diff, w_pool, pool_scale):
    b, t = diff.shape[:2]
    y = jnp.einsum('btgc,gce->btge', diff, w_pool.astype(jnp.float32)).reshape(b, t, D_MODEL)
    return y * pool_scale.astype(jnp.float32)


def moba_prompt(q, k, v):
    b, s, h, d = q.shape
    nb = -(-s // MOBA_BLOCK)
    s_pad = nb * MOBA_BLOCK
    k_sel = min(MOBA_TOP_K, nb)
    scale = HEAD_DIM ** -0.5
    qh = q.transpose(0, 2, 1, 3)
    pad = ((0, 0), (0, 0), (0, s_pad - s), (0, 0))
    kp = jnp.pad(k.transpose(0, 2, 1, 3), pad)
    vp = jnp.pad(v.transpose(0, 2, 1, 3), pad)
    kb = kp.reshape(b, h, nb, MOBA_BLOCK, d)
    vb = vp.reshape(b, h, nb, MOBA_BLOCK, d)
    kmean = jnp.mean(kb.astype(jnp.float32), axis=3)
    gather_blocks = jax.vmap(jax.vmap(lambda blocks, idx: blocks[idx]))
    blk = jnp.arange(nb)

    def chunk(c):
        qs = c * Q_CHUNK
        qc = lax.dynamic_slice_in_dim(qh, qs, Q_CHUNK, axis=2).astype(jnp.float32)
        qpos = qs + jnp.arange(Q_CHUNK)
        cb = qs // MOBA_BLOCK
        gate = jnp.einsum('bhqd,bhnd->bhqn', qc, kmean)
        gate = jnp.where(blk < cb, gate, NEG_INF)
        _, idx = lax.top_k(gate, k_sel)
        valid = idx < cb
        ks = gather_blocks(kb, idx).astype(jnp.float32)
        vs = gather_blocks(vb, idx).astype(jnp.float32)
        s_sel = jnp.einsum('bhqd,bhqnkd->bhqnk', qc, ks) * scale
        s_sel = jnp.where(valid[..., None], s_sel, NEG_INF).reshape(b, h, Q_CHUNK, k_sel * MOBA_BLOCK)
        k_own = lax.dynamic_slice_in_dim(kp, cb * MOBA_BLOCK, MOBA_BLOCK, axis=2).astype(jnp.float32)
        v_own = lax.dynamic_slice_in_dim(vp, cb * MOBA_BLOCK, MOBA_BLOCK, axis=2).astype(jnp.float32)
        kpos = cb * MOBA_BLOCK + jnp.arange(MOBA_BLOCK)
        s_own = jnp.einsum('bhqd,bhkd->bhqk', qc, k_own) * scale
        s_own = jnp.where(kpos[None, :] <= qpos[:, None], s_own, NEG_INF)
        p = jax.nn.softmax(jnp.concatenate([s_sel, s_own], axis=-1), axis=-1)
        p_sel = p[..., :k_sel * MOBA_BLOCK].reshape(b, h, Q_CHUNK, k_sel, MOBA_BLOCK)
        p_own = p[..., k_sel * MOBA_BLOCK:]
        return (jnp.einsum('bhqnk,bhqnkd->bhqd', p_sel, vs)
                + jnp.einsum('bhqk,bhkd->bhqd', p_own, v_own))

    outs = lax.map(chunk, jnp.arange(s // Q_CHUNK))
    return outs.transpose(1, 0, 3, 2, 4).reshape(b, s, h * d)


def moba_sample(q, k_new, v_new, cache_k, cache_v, page_table):
    db, t, h, d = q.shape
    n_pages = page_table.shape[1]
    past = n_pages * PAGE_SIZE
    bp = MOBA_BLOCK // PAGE_SIZE
    cb = past // MOBA_BLOCK
    off = past - cb * MOBA_BLOCK
    k_sel = min(MOBA_TOP_K, cb)
    scale = HEAD_DIM ** -0.5
    qh = q.transpose(0, 2, 1, 3).astype(jnp.float32)
    qpos = past + jnp.arange(t)
    scores, values = [], []
    if k_sel > 0:
        k_full = cache_k[page_table[:, :cb * bp]]
        kmean = jnp.mean(k_full.astype(jnp.float32).reshape(db, cb, MOBA_BLOCK, h, d), axis=2)
        gate = jnp.einsum('bhqd,bnhd->bhqn', qh, kmean)
        _, idx = lax.top_k(gate, k_sel)
        logical = idx[..., None] * bp + jnp.arange(bp)
        phys = page_table[jnp.arange(db)[:, None, None, None, None], logical]
        h_idx = jnp.arange(h)[None, :, None, None, None]
        ks = cache_k[phys, :, h_idx].astype(jnp.float32).reshape(db, h, t, k_sel * MOBA_BLOCK, d)
        vs = cache_v[phys, :, h_idx].astype(jnp.float32).reshape(db, h, t, k_sel * MOBA_BLOCK, d)
        scores.append(jnp.einsum('bhqd,bhqkd->bhqk', qh, ks) * scale)
        values.append(vs)
    own_pages = page_table[:, cb * bp:]
    k_past_own = cache_k[own_pages].reshape(db, off, h, d)
    v_past_own = cache_v[own_pages].reshape(db, off, h, d)
    k_own = jnp.concatenate([k_past_own, k_new], axis=1).transpose(0, 2, 1, 3).astype(jnp.float32)
    v_own = jnp.concatenate([v_past_own, v_new], axis=1).transpose(0, 2, 1, 3).astype(jnp.float32)
    kpos = cb * MOBA_BLOCK + jnp.arange(off + t)
    s_own = jnp.einsum('bhqd,bhkd->bhqk', qh, k_own) * scale
    s_own = jnp.where(kpos[None, :] <= qpos[:, None], s_own, NEG_INF)
    scores.append(s_own)
    p = jax.nn.softmax(jnp.concatenate(scores, axis=-1), axis=-1)
    n_sel = k_sel * MOBA_BLOCK
    o = jnp.einsum('bhqk,bhkd->bhqd', p[..., n_sel:], v_own)
    if k_sel > 0:
        o = o + jnp.einsum('bhqk,bhqkd->bhqd', p[..., :n_sel], values[0])
    return o.transpose(0, 2, 1, 3).reshape(db, t, h * d)


def merge_out(x, attn, pool, g_pool, g_attn, w_out, norm2_g, w_up, w_down):
    mixed = (jax.nn.sigmoid(g_pool.astype(jnp.float32)) * pool
             + jax.nn.sigmoid(g_attn.astype(jnp.float32)) * attn)
    hres = x + mixed.astype(x.dtype) @ w_out
    a = jnp.square(jax.nn.relu(rmsnorm(hres, norm2_g) @ w_up))
    return hres + a @ w_down


def setup_inputs(seed: int = 0) -> dict:
    key = jax.random.key(seed)
    ks = jax.random.split(key, 20)
    n_pages = PAST_LEN // PAGE_SIZE
    n_used = DEC_BATCH * n_pages
    n_phys = n_used + n_used // 4
    f32 = jnp.float32
    nrm = jax.random.normal
    x_prompt = nrm(ks[0], (BATCH, SEQ, D_MODEL), f32)
    x_sample = nrm(ks[1], (DEC_BATCH, DEC_SEQ, D_MODEL), f32)
    cache_k = nrm(ks[2], (DEPTH, n_phys, PAGE_SIZE, N_HEADS, HEAD_DIM), f32)
    cache_v = nrm(ks[3], (DEPTH, n_phys, PAGE_SIZE, N_HEADS, HEAD_DIM), f32)
    state_pool = nrm(ks[4], (DEPTH, DEC_BATCH, POOL_BUF, POOL_WIDTH), f32)
    page_table = jax.random.permutation(ks[5], n_phys)[:n_used].reshape(DEC_BATCH, n_pages).astype(jnp.int32)
    norm1_g = 1.0 + 0.1 * nrm(ks[6], (DEPTH, D_MODEL), f32)
    w_in = nrm(ks[7], (DEPTH, D_MODEL, IN_WIDTH), f32) * D_MODEL ** -0.5
    q_norm_g = 1.0 + 0.1 * nrm(ks[8], (DEPTH, HEAD_DIM), f32)
    k_norm_g = 1.0 + 0.1 * nrm(ks[9], (DEPTH, HEAD_DIM), f32)
    w_pool = nrm(ks[10], (DEPTH, N_POOL_GROUPS, POOL_GROUP_DIM, POOL_OUT_DIM), f32) * POOL_GROUP_DIM ** -0.5
    pool_scale = 1.0 + 0.1 * nrm(ks[11], (DEPTH, D_MODEL), f32)
    w_out = nrm(ks[12], (DEPTH, D_MODEL, D_MODEL), f32) * D_MODEL ** -0.5
    norm2_g = 1.0 + 0.1 * nrm(ks[13], (DEPTH, D_MODEL), f32)
    w_up = nrm(ks[14], (DEPTH, D_MODEL, D_FF), f32) * D_MODEL ** -0.5
    w_down = nrm(ks[15], (DEPTH, D_FF, D_MODEL), f32) * D_FF ** -0.5
    return {"x_prompt": x_prompt, "x_sample": x_sample, "cache_k": cache_k, "cache_v": cache_v,
            "state_pool": state_pool, "page_table": page_table, "norm1_g": norm1_g, "w_in": w_in,
            "q_norm_g": q_norm_g, "k_norm_g": k_norm_g, "w_pool": w_pool, "pool_scale": pool_scale,
            "w_out": w_out, "norm2_g": norm2_g, "w_up": w_up, "w_down": w_down}


def reference(x_prompt, x_sample, cache_k, cache_v, state_pool, page_table, norm1_g, w_in,
              q_norm_g, k_norm_g, w_pool, pool_scale, w_out, norm2_g, w_up, w_down):
    past = page_table.shape[1] * PAGE_SIZE
    xp, xs = x_prompt, x_sample
    kp_l, vp_l, pp_l, ks_l, vs_l, ps_l = [], [], [], [], [], []
    for l in range(DEPTH):
        q, k, v, u, g_pool, g_attn = project_in(xp, norm1_g[l], w_in[l], q_norm_g[l], k_norm_g[l])
        attn = moba_prompt(q, k, v)
        pool = pool_branch(multiscale_pool(u, 0, 0), w_pool[l], pool_scale[l])
        xp = merge_out(xp, attn, pool, g_pool, g_attn, w_out[l], norm2_g[l], w_up[l], w_down[l])
        kp_l.append(k)
        vp_l.append(v)
        pp_l.append(u[:, -POOL_BUF:])
        q, k, v, u, g_pool, g_attn = project_in(xs, norm1_g[l], w_in[l], q_norm_g[l], k_norm_g[l])
        attn = moba_sample(q, k, v, cache_k[l], cache_v[l], page_table)
        u_ext = jnp.concatenate([state_pool[l].astype(u.dtype), u], axis=1)
        pool = pool_branch(multiscale_pool(u_ext, POOL_BUF, past - POOL_BUF), w_pool[l], pool_scale[l])
        xs = merge_out(xs, attn, pool, g_pool, g_attn, w_out[l], norm2_g[l], w_up[l], w_down[l])
        ks_l.append(k)
        vs_l.append(v)
        ps_l.append(u_ext[:, -POOL_BUF:])
    k_prompt = jnp.stack(kp_l, axis=0)
    v_prompt = jnp.stack(vp_l, axis=0)
    pool_prompt = jnp.stack(pp_l, axis=0)
    k_sample = jnp.stack(ks_l, axis=0)
    v_sample = jnp.stack(vs_l, axis=0)
    pool_sample = jnp.stack(ps_l, axis=0)
    return (xp, xs, k_prompt, v_prompt, pool_prompt, k_sample, v_sample, pool_sample)
```

```python
import functools

import jax
import jax.numpy as jnp
from jax import lax
from jax.experimental import pallas as pl
from jax.experimental.pallas import tpu as pltpu

PAGE_SIZE = 128
N_HEADS = 8
MOBA_BLOCK = 256
MOBA_TOP_K = 3
POOL_WINDOWS = (2, 4, 8, 16)
N_POOL_GROUPS = len(POOL_WINDOWS)
EPS = 1e-6
NEG_INF = -1e30
PICKED = -3e38

V7X_LANES = 128
V7X_SUBLANES = 8
V7X_VMEM_BYTES = 64 * 1024 * 1024

POOL_HALO = 16
FF_CHUNK = 1024
SAMPLE_PAGE_BUFFERS = 8

F32 = jnp.float32
BF16 = jnp.bfloat16


def _vmem_limit(pipelined_bytes, resident_bytes, live_bytes):
    est = 2 * pipelined_bytes + resident_bytes + live_bytes
    return int(min(max(est, 16 * 1024 * 1024), V7X_VMEM_BYTES - 8 * 1024 * 1024))


def _nbytes(shape, dtype):
    n = 1
    for s in shape:
        n *= s
    return n * jnp.dtype(dtype).itemsize


def _rms_scale(x):
    return lax.rsqrt(jnp.mean(x * x, axis=-1, keepdims=True) + EPS)


def _dot_nt(a, b, precision=None):
    return lax.dot_general(a, b, (((1,), (1,)), ((), ())), precision=precision,
                           preferred_element_type=F32)


def _project_in_kernel(x_ref, g1_ref, w_ref, qg_ref, kg_ref, *out_refs, d_model, prompt_extras):
    if prompt_extras:
        q_ref, k_ref, v_ref, u_ref, gp_ref, ga_ref, kb_ref, vb_ref, ksum_ref = out_refs
    else:
        q_ref, k_ref, v_ref, u_ref, gp_ref, ga_ref = out_refs
    dh = d_model // N_HEADS
    pool_w = u_ref.shape[-1]
    o1, o2, o3 = d_model, 2 * d_model, 3 * d_model
    o4 = o3 + pool_w
    o5 = o4 + d_model
    o6 = o5 + d_model

    x = x_ref[...]
    xn = (x * _rms_scale(x) * g1_ref[...]).astype(BF16)

    def proj(c0, c1):
        return jnp.dot(xn, w_ref[:, c0:c1], preferred_element_type=F32)

    zq = proj(0, o1)
    for h in range(N_HEADS):
        sl = slice(h * dh, (h + 1) * dh)
        zh = zq[:, sl]
        q_ref[:, sl] = zh * _rms_scale(zh) * qg_ref[...]
    zk = proj(o1, o2)
    for h in range(N_HEADS):
        sl = slice(h * dh, (h + 1) * dh)
        zh = zk[:, sl]
        kh = zh * _rms_scale(zh) * kg_ref[...]
        k_ref[:, sl] = kh
        if prompt_extras:
            kb_ref[:, sl] = kh.astype(BF16)
            ksum_ref[:, sl] = jnp.sum(kh, axis=0, keepdims=True)
    zv = proj(o2, o3)
    v_ref[...] = zv
    if prompt_extras:
        vb_ref[...] = zv.astype(BF16)
    u_ref[...] = proj(o3, o4)
    gp_ref[...] = proj(o4, o5)
    ga_ref[...] = proj(o5, o6)


def _project_in(x2d, g1, w_bf16, qg, kg, *, row_tile, prompt_extras):
    n, d = x2d.shape
    in_w = w_bf16.shape[1]
    pool_w = in_w - 5 * d
    dh = d // N_HEADS
    assert n % row_tile == 0
    n_tiles = n // row_tile
    row = lambda width: pl.BlockSpec((row_tile, width), lambda i: (i, 0))
    const = lambda shape: pl.BlockSpec(shape, lambda i: (0, 0))
    out_shape = [jax.ShapeDtypeStruct((n, d), F32)] * 3 + [jax.ShapeDtypeStruct((n, pool_w), F32)] \
        + [jax.ShapeDtypeStruct((n, d), F32)] * 2
    out_specs = [row(d)] * 3 + [row(pool_w)] + [row(d)] * 2
    tile_bytes = _nbytes((row_tile, d), F32) * 6 + _nbytes((row_tile, pool_w), F32)
    if prompt_extras:
        out_shape += [jax.ShapeDtypeStruct((n, d), BF16)] * 2 + [jax.ShapeDtypeStruct((n_tiles, 1, d), F32)]
        out_specs += [row(d)] * 2 + [pl.BlockSpec((None, 1, d), lambda i: (i, 0, 0))]
        tile_bytes += 2 * _nbytes((row_tile, d), BF16)
    return pl.pallas_call(
        functools.partial(_project_in_kernel, d_model=d, prompt_extras=prompt_extras),
        grid=(n_tiles,),
        in_specs=[row(d), const((1, d)), const((d, in_w)), const((1, dh)), const((1, dh))],
        out_specs=out_specs,
        out_shape=out_shape,
        compiler_params=pltpu.CompilerParams(
            dimension_semantics=("arbitrary",),
            vmem_limit_bytes=_vmem_limit(tile_bytes, 2 * _nbytes((d, in_w), BF16),
                                         4 * _nbytes((row_tile, d), F32))),
    )(x2d, g1, w_bf16, qg, kg)


def _moba_prompt_kernel(q_ref, kb_ref, vb_ref, ksum_ref, o_ref, kaug_ref, *, scale):
    i = pl.program_id(2)
    seq, dh = kb_ref.shape
    n_blocks = seq // MOBA_BLOCK
    tq = q_ref.shape[0]

    @pl.when(i == 0)
    def _():
        kaug_ref[:, :dh] = kb_ref[...]
        row_blk = lax.broadcasted_iota(jnp.int32, (seq, V7X_LANES), 0) // MOBA_BLOCK
        lane = lax.broadcasted_iota(jnp.int32, (seq, V7X_LANES), 1)
        kaug_ref[:, dh:] = jnp.where(row_blk == lane, 1.0, 0.0).astype(BF16)

    q = q_ref[...]
    kmean = ksum_ref[...] * (1.0 / MOBA_BLOCK)
    gate = _dot_nt(kmean, q, precision=lax.Precision.HIGHEST)
    blk = lax.broadcasted_iota(jnp.int32, gate.shape, 0)
    past = blk < i
    g = jnp.where(past, gate, NEG_INF)
    sel = jnp.zeros(gate.shape, jnp.bool_)
    for _ in range(MOBA_TOP_K):
        m = jnp.max(g, axis=0, keepdims=True)
        first = jnp.min(jnp.where(g == m, blk, n_blocks), axis=0, keepdims=True)
        pick = blk == first
        sel = jnp.logical_or(sel, pick)
        g = jnp.where(pick, PICKED, g)
    bias_t = jnp.where(jnp.logical_and(sel, past), 0.0, NEG_INF)
    bias_t = jnp.concatenate([bias_t, jnp.zeros((V7X_LANES - n_blocks, tq), F32)], axis=0)
    bias = bias_t.T
    qs = (q * scale).astype(BF16)
    q_aug = jnp.concatenate([qs, bias.astype(BF16)], axis=1)

    own = pl.multiple_of(i * MOBA_BLOCK, MOBA_BLOCK)
    s = _dot_nt(qs, kb_ref[pl.ds(own, MOBA_BLOCK), :])
    qi = lax.broadcasted_iota(jnp.int32, s.shape, 0)
    ki = lax.broadcasted_iota(jnp.int32, s.shape, 1)
    s = jnp.where(ki <= qi, s, NEG_INF)
    m0 = jnp.max(s, axis=1, keepdims=True)
    p = jnp.exp(s - m0)
    l0 = jnp.sum(p, axis=1, keepdims=True)
    acc0 = jnp.dot(p.astype(BF16), vb_ref[pl.ds(own, MOBA_BLOCK), :], preferred_element_type=F32)

    def past_block(j, carry):
        m_run, l_run, acc = carry
        start = pl.multiple_of(j * MOBA_BLOCK, MOBA_BLOCK)
        s = _dot_nt(q_aug, kaug_ref[pl.ds(start, MOBA_BLOCK), :])
        m_new = jnp.maximum(m_run, jnp.max(s, axis=1, keepdims=True))
        alpha = jnp.exp(m_run - m_new)
        p = jnp.exp(s - m_new)
        l_new = alpha * l_run + jnp.sum(p, axis=1, keepdims=True)
        acc_new = alpha * acc + jnp.dot(p.astype(BF16), vb_ref[pl.ds(start, MOBA_BLOCK), :],
                                        preferred_element_type=F32)
        return m_new, l_new, acc_new

    _, l_fin, acc_fin = lax.fori_loop(0, i, past_block, (m0, l0, acc0))
    o_ref[...] = acc_fin / l_fin


def _moba_prompt(q, kb, vb, ksum, *, scale):
    b, s, d = q.shape
    dh = d // N_HEADS
    assert s % MOBA_BLOCK == 0 and dh == V7X_LANES
    n_blocks = s // MOBA_BLOCK
    assert n_blocks <= V7X_LANES
    tq = MOBA_BLOCK
    kv_spec = pl.BlockSpec((None, s, dh), lambda bi, h, i: (bi, 0, h))
    tile = pl.BlockSpec((None, tq, dh), lambda bi, h, i: (bi, i, h))
    return pl.pallas_call(
        functools.partial(_moba_prompt_kernel, scale=scale),
        grid=(b, N_HEADS, n_blocks),
        in_specs=[tile, kv_spec, kv_spec,
                  pl.BlockSpec((None, n_blocks, dh), lambda bi, h, i: (bi, 0, h))],
        out_specs=tile,
        out_shape=jax.ShapeDtypeStruct((b, s, d), F32),
        scratch_shapes=[pltpu.VMEM((s, dh + V7X_LANES), BF16)],
        compiler_params=pltpu.CompilerParams(
            dimension_semantics=("arbitrary", "arbitrary", "arbitrary"),
            vmem_limit_bytes=_vmem_limit(
                2 * _nbytes((s, dh), BF16) + 2 * _nbytes((tq, dh), F32),
                _nbytes((s, dh + V7X_LANES), BF16),
                16 * _nbytes((tq, MOBA_BLOCK), F32))),
    )(q, kb, vb, ksum)


def _sample_select_kernel(pt_ref, q_ref, ck_ref, idx_ref, kbuf, sem, kmean_ref, *,
                          n_blocks, pages_per_block, n_buf):
    b = pl.program_id(0)
    n_pages = n_blocks * pages_per_block
    t_new, n_heads, dh = q_ref.shape

    def page_copy(p, slot):
        return pltpu.make_async_copy(ck_ref.at[pt_ref[b, p]], kbuf.at[slot], sem.at[slot])

    for p in range(n_buf):
        page_copy(p, p).start()

    def block_mean(n, carry):
        acc = jnp.zeros((n_heads, dh), F32)
        for pp in range(pages_per_block):
            p = n * pages_per_block + pp
            slot = p % n_buf
            page_copy(p, slot).wait()
            acc = acc + jnp.sum(kbuf[slot], axis=0)

            @pl.when(p + n_buf < n_pages)
            def _():
                page_copy(p + n_buf, slot).start()
        kmean_ref[n] = acc * (1.0 / MOBA_BLOCK)
        return carry

    lax.fori_loop(0, n_blocks, block_mean, 0)

    kmean = kmean_ref[...]
    blk = lax.broadcasted_iota(jnp.int32, (n_blocks, n_heads, 1), 0).astype(F32)
    lane = lax.broadcasted_iota(jnp.int32, (n_heads, V7X_LANES), 1)
    for t in range(t_new):
        g = jnp.sum(kmean * q_ref[t][None], axis=-1, keepdims=True)
        tile = jnp.zeros((n_heads, V7X_LANES), F32)
        for r in range(MOBA_TOP_K):
            m = jnp.max(g, axis=0, keepdims=True)
            first = jnp.min(jnp.where(g == m, blk, float(n_blocks)), axis=0, keepdims=True)
            tile = jnp.where(lane == r, first[0], tile)
            g = jnp.where(blk == first, PICKED, g)
        idx_ref[t] = tile.astype(jnp.int32)


def _sample_select(page_table, q4, cache_k):
    db, t_new, n_heads, dh = q4.shape
    n_pages = page_table.shape[1]
    pages_per_block = MOBA_BLOCK // PAGE_SIZE
    assert n_pages % pages_per_block == 0, "cached rows must fill whole MoBA blocks"
    n_blocks = n_pages // pages_per_block
    assert n_blocks >= MOBA_TOP_K and n_heads == V7X_SUBLANES and dh == V7X_LANES
    n_buf = min(SAMPLE_PAGE_BUFFERS, n_pages)
    page = (PAGE_SIZE, n_heads, dh)
    grid_spec = pltpu.PrefetchScalarGridSpec(
        num_scalar_prefetch=1,
        grid=(db,),
        in_specs=[pl.BlockSpec((None, t_new, n_heads, dh), lambda b, pt: (b, 0, 0, 0)),
                  pl.BlockSpec(memory_space=pl.ANY)],
        out_specs=pl.BlockSpec((None, t_new, n_heads, V7X_LANES), lambda b, pt: (b, 0, 0, 0)),
        scratch_shapes=[pltpu.VMEM((n_buf,) + page, F32),
                        pltpu.SemaphoreType.DMA((n_buf,)),
                        pltpu.VMEM((n_blocks, n_heads, dh), F32)])
    return pl.pallas_call(
        functools.partial(_sample_select_kernel, n_blocks=n_blocks,
                          pages_per_block=pages_per_block, n_buf=n_buf),
        grid_spec=grid_spec,
        out_shape=jax.ShapeDtypeStruct((db, t_new, n_heads, V7X_LANES), jnp.int32),
        compiler_params=pltpu.CompilerParams(
            dimension_semantics=("arbitrary",),
            vmem_limit_bytes=_vmem_limit(
                2 * _nbytes((t_new, n_heads, V7X_LANES), F32),
                _nbytes((n_buf,) + page, F32) + _nbytes((n_blocks, n_heads, dh), F32),
                4 * _nbytes(page, F32))),
    )(page_table, q4, cache_k)


def _sample_attend_kernel(pt_ref, sel_ref, q_ref, kn_ref, vn_ref, ck_ref, cv_ref, o_ref,
                          kg, vg, sem, *, pages_per_block, scale):
    b = pl.program_id(0)
    t_new, d = q_ref.shape
    dh = d // N_HEADS
    rows_per_token = MOBA_TOP_K * MOBA_BLOCK

    def gather_copies(h, slot):
        copies = []
        for t in range(t_new):
            for r in range(MOBA_TOP_K):
                blk = sel_ref[b, (t * N_HEADS + h) * MOBA_TOP_K + r]
                for pp in range(pages_per_block):
                    phys = pt_ref[b, blk * pages_per_block + pp]
                    row0 = ((t * MOBA_TOP_K + r) * pages_per_block + pp) * PAGE_SIZE
                    dst = pl.ds(row0, PAGE_SIZE)
                    copies.append(pltpu.make_async_copy(
                        ck_ref.at[phys, :, h, :], kg.at[slot, dst, :], sem.at[0, slot]))
                    copies.append(pltpu.make_async_copy(
                        cv_ref.at[phys, :, h, :], vg.at[slot, dst, :], sem.at[1, slot]))
        return copies

    q = q_ref[...]
    kn = kn_ref[...]
    vn = vn_ref[...]
    own_row = lax.broadcasted_iota(jnp.int32, (t_new, 1), 0)

    pending = gather_copies(0, 0)
    for c in pending:
        c.start()
    for h in range(N_HEADS):
        slot = h % 2
        current = pending
        if h + 1 < N_HEADS:
            pending = gather_copies(h + 1, 1 - slot)
            for c in pending:
                c.start()
        for c in current:
            c.wait()
        sl = slice(h * dh, (h + 1) * dh)
        kn_h = kn[:, sl]
        vn_h = vn[:, sl]
        for t in range(t_new):
            rows = pl.ds(t * rows_per_token, rows_per_token)
            qrow = q[t:t + 1, sl] * scale
            s = jnp.sum(kg[slot, rows, :] * qrow, axis=-1, keepdims=True)
            s_own = jnp.sum(kn_h * qrow, axis=-1, keepdims=True)
            s_own = jnp.where(own_row <= t, s_own, NEG_INF)
            m = jnp.maximum(jnp.max(s, axis=0, keepdims=True), jnp.max(s_own, axis=0, keepdims=True))
            p = jnp.exp(s - m)
            p_own = jnp.exp(s_own - m)
            denom = jnp.sum(p, axis=0, keepdims=True) + jnp.sum(p_own, axis=0, keepdims=True)
            o = (jnp.sum(p * vg[slot, rows, :], axis=0, keepdims=True)
                 + jnp.sum(p_own * vn_h, axis=0, keepdims=True))
            o_ref[t:t + 1, sl] = o / denom


def _sample_attend(page_table, sel, q3, kn3, vn3, cache_k, cache_v, *, scale):
    db, t_new, d = q3.shape
    dh = d // N_HEADS
    pages_per_block = MOBA_BLOCK // PAGE_SIZE
    gathered = (2, t_new * MOBA_TOP_K * MOBA_BLOCK, dh)
    tok = pl.BlockSpec((None, t_new, d), lambda b, pt, sel: (b, 0, 0))
    grid_spec = pltpu.PrefetchScalarGridSpec(
        num_scalar_prefetch=2,
        grid=(db,),
        in_specs=[tok, tok, tok, pl.BlockSpec(memory_space=pl.ANY), pl.BlockSpec(memory_space=pl.ANY)],
        out_specs=tok,
        scratch_shapes=[pltpu.VMEM(gathered, F32), pltpu.VMEM(gathered, F32),
                        pltpu.SemaphoreType.DMA((2, 2))])
    return pl.pallas_call(
        functools.partial(_sample_attend_kernel, pages_per_block=pages_per_block, scale=scale),
        grid_spec=grid_spec,
        out_shape=jax.ShapeDtypeStruct((db, t_new, d), F32),
        compiler_params=pltpu.CompilerParams(
            dimension_semantics=("arbitrary",),
            vmem_limit_bytes=_vmem_limit(4 * _nbytes((V7X_SUBLANES, d), F32),
                                         2 * _nbytes(gathered, F32),
                                         8 * _nbytes((MOBA_TOP_K * MOBA_BLOCK, dh), F32))),
    )(page_table, sel, q3, kn3, vn3, cache_k, cache_v)


def _pool_mix_kernel(u_ref, halo_ref, gp_ref, ga_ref, attn_ref, wp_ref, ps_ref, o_ref, uext, *,
                     base_pos):
    t_rows, pool_w = u_ref.shape
    t_pad = uext.shape[0] - POOL_HALO
    gdim = pool_w // N_POOL_GROUPS
    gout = wp_ref.shape[-1]
    pos0 = base_pos + pl.program_id(1) * t_rows

    halo_pos = pos0 - POOL_HALO + lax.broadcasted_iota(jnp.int32, (POOL_HALO, 1), 0)
    uext[0:POOL_HALO, :] = jnp.where(halo_pos >= 0, halo_ref[...], 0.0)
    uext[POOL_HALO:POOL_HALO + t_rows, :] = u_ref[...]
    if t_pad > t_rows:
        uext[POOL_HALO + t_rows:, :] = jnp.zeros((t_pad - t_rows, pool_w), F32)

    pos = pos0 + lax.broadcasted_iota(jnp.int32, (t_pad, 1), 0)
    for g, window in enumerate(POOL_WINDOWS):
        cols = slice(g * gdim, (g + 1) * gdim)
        tok = uext[POOL_HALO:POOL_HALO + t_pad, cols]
        total = tok
        for r in range(1, window):
            total = total + uext[POOL_HALO - r:POOL_HALO - r + t_pad, cols]
        cnt = jnp.minimum(window, pos + 1).astype(F32)
        diff = total / cnt - tok
        y = jnp.dot(diff.astype(BF16), wp_ref[g], preferred_element_type=F32)[:t_rows]
        oc = slice(g * gout, (g + 1) * gout)
        pool = y * ps_ref[:, oc]
        o_ref[:, oc] = (jax.nn.sigmoid(gp_ref[:, oc]) * pool
                        + jax.nn.sigmoid(ga_ref[:, oc]) * attn_ref[:, oc]).astype(o_ref.dtype)


def _pool_mix(u3, halo3, halo_index, gp3, ga3, attn3, w_pool_bf16, pool_scale, *, row_tile,
              base_pos, out_dtype):
    nseq, s, pool_w = u3.shape
    d = gp3.shape[-1]
    assert s % row_tile == 0
    t_pad = -(-row_tile // V7X_SUBLANES) * V7X_SUBLANES
    wide = pl.BlockSpec((None, row_tile, d), lambda q, i: (q, i, 0))
    return pl.pallas_call(
        functools.partial(_pool_mix_kernel, base_pos=base_pos),
        grid=(nseq, s // row_tile),
        in_specs=[pl.BlockSpec((None, row_tile, pool_w), lambda q, i: (q, i, 0)),
                  pl.BlockSpec((None, POOL_HALO, pool_w), halo_index),
                  wide, wide, wide,
                  pl.BlockSpec(w_pool_bf16.shape, lambda q, i: (0, 0, 0)),
                  pl.BlockSpec((1, d), lambda q, i: (0, 0))],
        out_specs=wide,
        out_shape=jax.ShapeDtypeStruct((nseq, s, d), out_dtype),
        scratch_shapes=[pltpu.VMEM((POOL_HALO + t_pad, pool_w), F32)],
        compiler_params=pltpu.CompilerParams(
            dimension_semantics=("arbitrary", "arbitrary"),
            vmem_limit_bytes=_vmem_limit(5 * _nbytes((t_pad, d), F32),
                                         _nbytes(w_pool_bf16.shape, BF16),
                                         6 * _nbytes((t_pad, d), F32))),
    )(u3, halo3, gp3, ga3, attn3, w_pool_bf16, pool_scale)


def _out_mlp_kernel(x_ref, mixed_ref, wo_ref, g2_ref, wu_ref, wd_ref, y_ref):
    d_ff = wu_ref.shape[1]
    hres = x_ref[...] + jnp.dot(mixed_ref[...].astype(BF16), wo_ref[...], preferred_element_type=F32)
    hn = (hres * _rms_scale(hres) * g2_ref[...]).astype(BF16)
    acc = hres
    for c in range(d_ff // FF_CHUNK):
        cols = slice(c * FF_CHUNK, (c + 1) * FF_CHUNK)
        a = jnp.dot(hn, wu_ref[:, cols], preferred_element_type=F32)
        a = jnp.square(jnp.maximum(a, 0.0)).astype(BF16)
        acc = acc + jnp.dot(a, wd_ref[cols, :], preferred_element_type=F32)
    y_ref[...] = acc


def _out_mlp(x2d, mixed2d, w_out, g2, w_up, w_down, *, row_tile):
    n, d = x2d.shape
    d_ff = w_up.shape[1]
    assert n % row_tile == 0 and d_ff % FF_CHUNK == 0
    row = pl.BlockSpec((row_tile, d), lambda i: (i, 0))
    const = lambda shape: pl.BlockSpec(shape, lambda i: (0, 0))
    weights = _nbytes((d, d), BF16) + 2 * _nbytes((d, d_ff), BF16)
    return pl.pallas_call(
        _out_mlp_kernel,
        grid=(n // row_tile,),
        in_specs=[row, row, const((d, d)), const((1, d)), const((d, d_ff)), const((d_ff, d))],
        out_specs=row,
        out_shape=jax.ShapeDtypeStruct((n, d), F32),
        compiler_params=pltpu.CompilerParams(
            dimension_semantics=("arbitrary",),
            vmem_limit_bytes=_vmem_limit(3 * _nbytes((row_tile, d), F32), 2 * weights,
                                         6 * _nbytes((row_tile, FF_CHUNK), F32))),
    )(x2d, mixed2d, w_out, g2, w_up, w_down)


def kernel(x_prompt, x_sample, cache_k, cache_v, state_pool, page_table, norm1_g, w_in,
           q_norm_g, k_norm_g, w_pool, pool_scale, w_out, norm2_g, w_up, w_down):
    depth = w_in.shape[0]
    b, s, d = x_prompt.shape
    db, t_new, _ = x_sample.shape
    dh = d // N_HEADS
    n_pages = page_table.shape[1]
    past = n_pages * PAGE_SIZE
    scale = dh ** -0.5
    pool_buf = state_pool.shape[2]
    pool_w = state_pool.shape[3]
    assert pool_buf == POOL_HALO - 1
    n_blocks = s // MOBA_BLOCK

    xp = x_prompt.reshape(b * s, d)
    xs = x_sample.reshape(db * t_new, d)
    outs = [[] for _ in range(6)]
    for l in range(depth):
        w_in_l = w_in[l].astype(BF16)
        w_pool_l = w_pool[l].astype(BF16)
        w_out_l = w_out[l].astype(BF16)
        w_up_l = w_up[l].astype(BF16)
        w_down_l = w_down[l].astype(BF16)
        g1, g2 = norm1_g[l][None], norm2_g[l][None]
        qg, kg = q_norm_g[l][None], k_norm_g[l][None]
        ps = pool_scale[l][None]

        q, k, v, u, gp, ga, kb, vb, ksum = _project_in(
            xp, g1, w_in_l, qg, kg, row_tile=MOBA_BLOCK, prompt_extras=True)
        shape3 = lambda a: a.reshape(b, s, a.shape[-1])
        attn = _moba_prompt(shape3(q), shape3(kb), shape3(vb), ksum.reshape(b, n_blocks, d), scale=scale)
        tile = MOBA_BLOCK
        halo_blocks = tile // POOL_HALO
        mixed = _pool_mix(
            shape3(u), shape3(u), lambda qi, i: (qi, jnp.maximum(i * halo_blocks - 1, 0), 0),
            shape3(gp), shape3(ga), attn, w_pool_l, ps, row_tile=tile, base_pos=0, out_dtype=BF16)
        xp = _out_mlp(xp, mixed.reshape(b * s, d), w_out_l, g2, w_up_l, w_down_l, row_tile=tile)
        outs[0].append(k.reshape(b, s, N_HEADS, dh))
        outs[1].append(v.reshape(b, s, N_HEADS, dh))
        outs[2].append(shape3(u)[:, s - pool_buf:])

        q, k, v, u, gp, ga = _project_in(
            xs, g1, w_in_l, qg, kg, row_tile=db * t_new, prompt_extras=False)
        tok3 = lambda a: a.reshape(db, t_new, a.shape[-1])
        picked = _sample_select(page_table, q.reshape(db, t_new, N_HEADS, dh), cache_k[l])
        sel = picked[:, :, :, :MOBA_TOP_K].reshape(db, t_new * N_HEADS * MOBA_TOP_K)
        attn = _sample_attend(page_table, sel, tok3(q), tok3(k), tok3(v), cache_k[l], cache_v[l],
                              scale=scale)
        state = state_pool[l].astype(F32)
        halo = jnp.pad(state, ((0, 0), (POOL_HALO - pool_buf, 0), (0, 0)))
        mixed = _pool_mix(
            tok3(u), halo, lambda qi, i: (qi, 0, 0), tok3(gp), tok3(ga), attn, w_pool_l, ps,
            row_tile=t_new, base_pos=past, out_dtype=F32)
        xs = _out_mlp(xs, mixed.reshape(db * t_new, d), w_out_l, g2, w_up_l, w_down_l,
                      row_tile=db * t_new)
        outs[3].append(k.reshape(db, t_new, N_HEADS, dh))
        outs[4].append(v.reshape(db, t_new, N_HEADS, dh))
        outs[5].append(jnp.concatenate([state, tok3(u)], axis=1)[:, -pool_buf:])

    stacked = [jnp.stack(o, axis=0) for o in outs]
    return (xp.reshape(b, s, d), xs.reshape(db, t_new, d), *stacked)
```

```python
import functools

import jax
import jax.numpy as jnp
from jax import lax
from jax.experimental import pallas as pl
from jax.experimental.pallas import tpu as pltpu

PAGE_SIZE = 128
N_HEADS = 8
MOBA_BLOCK = 256
MOBA_TOP_K = 3
POOL_WINDOWS = (2, 4, 8, 16)
N_POOL_GROUPS = len(POOL_WINDOWS)
EPS = 1e-6
NEG_INF = -1e30
PICKED = -3e38
LOG2_E = 1.4426950408889634

V7X_LANES = 128
V7X_SUBLANES = 8
V7X_VMEM_BYTES = 64 * 1024 * 1024

POOL_HALO = 16
FF_CHUNK = 1024
HEAD_BIAS_LANES = V7X_LANES // N_HEADS
MOBA_QUAD = 4
PARTIAL_SUMS = 8

F32 = jnp.float32
BF16 = jnp.bfloat16


def _vmem_limit(pipelined_bytes, resident_bytes, live_bytes):
    est = 2 * pipelined_bytes + resident_bytes + live_bytes
    return int(min(max(est, 16 * 1024 * 1024), V7X_VMEM_BYTES - 8 * 1024 * 1024))


def _nbytes(shape, dtype):
    n = 1
    for s in shape:
        n *= s
    return n * jnp.dtype(dtype).itemsize


def _rms_scale(x):
    return lax.rsqrt(jnp.mean(x * x, axis=-1, keepdims=True) + EPS)


def _dot_nt(a, b, precision=None):
    return lax.dot_general(a, b, (((1,), (1,)), ((), ())), precision=precision,
                           preferred_element_type=F32)


def _dot_nt_3pass(a, b):
    a_hi = a.astype(BF16)
    b_hi = b.astype(BF16)
    a_lo = (a - a_hi.astype(F32)).astype(BF16)
    b_lo = (b - b_hi.astype(F32)).astype(BF16)
    return _dot_nt(a_hi, b_hi) + (_dot_nt(a_hi, b_lo) + _dot_nt(a_lo, b_hi))


def _project_in_kernel(x_ref, g1_ref, w_ref, qg_ref, kg_ref, *refs, d_model, prompt_extras, scale):
    if prompt_extras:
        k_ref, v_ref, u_ref, gp_ref, ga_ref, qs_ref, bias_ref, kb_ref, vt_ref, kmean_ref = refs
    else:
        q_ref, k_ref, v_ref, u_ref, gp_ref, ga_ref = refs
    dh = d_model // N_HEADS
    pool_w = u_ref.shape[-1]
    rows = x_ref.shape[0]
    o1, o2, o3 = d_model, 2 * d_model, 3 * d_model
    o4 = o3 + pool_w
    o5 = o4 + d_model
    o6 = o5 + d_model

    x = x_ref[...]
    xn = (x * _rms_scale(x) * g1_ref[...]).astype(BF16)

    def proj(c0, c1):
        return jnp.dot(xn, w_ref[:, c0:c1], preferred_element_type=F32)

    if prompt_extras:
        i = pl.program_id(1)

        @pl.when(jnp.logical_and(pl.program_id(0) == 0, i == 0))
        def _():
            kmean_ref[...] = jnp.zeros(kmean_ref.shape, F32)

    zq = proj(0, o1)
    gates = []
    for h in range(N_HEADS):
        sl = slice(h * dh, (h + 1) * dh)
        zh = zq[:, sl]
        qh = zh * _rms_scale(zh) * qg_ref[...]
        if prompt_extras:
            qs_ref[:, sl] = (qh * (scale * LOG2_E)).astype(BF16)
            gates.append(_dot_nt_3pass(kmean_ref[:, sl], qh))
        else:
            q_ref[:, sl] = qh

    if prompt_extras:
        g = jnp.stack(gates, axis=0)
        blk = lax.broadcasted_iota(jnp.int32, g.shape, 1)
        past = blk < i
        g = jnp.where(past, g, NEG_INF)
        sel = jnp.zeros(g.shape, jnp.bool_)
        for _ in range(MOBA_TOP_K):
            m = jnp.max(g, axis=1, keepdims=True)
            first = jnp.min(jnp.where(g == m, blk, HEAD_BIAS_LANES), axis=1, keepdims=True)
            pick = blk == first
            sel = jnp.logical_or(sel, pick)
            g = jnp.where(pick, PICKED, g)
        bias_t = jnp.where(jnp.logical_and(sel, past), 0.0, NEG_INF)
        bias_ref[...] = bias_t.reshape(V7X_LANES, rows).T.astype(BF16)

    zk = proj(o1, o2)
    for h in range(N_HEADS):
        sl = slice(h * dh, (h + 1) * dh)
        zh = zk[:, sl]
        kh = zh * _rms_scale(zh) * kg_ref[...]
        k_ref[:, sl] = kh
        if prompt_extras:
            kb_ref[:, sl] = kh.astype(BF16)
            block_row = lax.broadcasted_iota(jnp.int32, (HEAD_BIAS_LANES, dh), 0) == i
            kmean_ref[:, sl] = jnp.where(
                block_row, jnp.sum(kh, axis=0, keepdims=True) * (1.0 / MOBA_BLOCK), kmean_ref[:, sl])
    zv = proj(o2, o3)
    v_ref[...] = zv
    if prompt_extras:
        vt_ref[...] = zv.T.astype(BF16)
    u_ref[...] = proj(o3, o4)
    gp_ref[...] = proj(o4, o5)
    ga_ref[...] = proj(o5, o6)


def _project_in(x3, g1, w_bf16, qg, kg, *, row_tile, prompt_extras, scale):
    nseq, s, d = x3.shape
    in_w = w_bf16.shape[1]
    pool_w = in_w - 5 * d
    dh = d // N_HEADS
    assert s % row_tile == 0
    tiles = s // row_tile
    row = lambda width: pl.BlockSpec((None, row_tile, width), lambda q, i: (q, i, 0))
    const = lambda shape: pl.BlockSpec(shape, lambda q, i: (0, 0))
    wide = jax.ShapeDtypeStruct((nseq, s, d), F32)
    scratch = []
    if prompt_extras:
        assert row_tile == MOBA_BLOCK and tiles <= HEAD_BIAS_LANES and dh == V7X_LANES
        out_shape = [wide, wide, jax.ShapeDtypeStruct((nseq, s, pool_w), F32), wide, wide,
                     jax.ShapeDtypeStruct((nseq, s, d), BF16),
                     jax.ShapeDtypeStruct((nseq, s, V7X_LANES), BF16),
                     jax.ShapeDtypeStruct((nseq, s, d), BF16),
                     jax.ShapeDtypeStruct((nseq, d, s), BF16)]
        out_specs = [row(d), row(d), row(pool_w), row(d), row(d), row(d), row(V7X_LANES), row(d),
                     pl.BlockSpec((None, d, row_tile), lambda q, i: (q, 0, i))]
        tile_bytes = (_nbytes((row_tile, d), F32) * 5 + _nbytes((row_tile, pool_w), F32)
                      + 4 * _nbytes((row_tile, d), BF16))
        scratch = [pltpu.VMEM((HEAD_BIAS_LANES, d), F32)]
    else:
        out_shape = [wide] * 3 + [jax.ShapeDtypeStruct((nseq, s, pool_w), F32)] + [wide] * 2
        out_specs = [row(d)] * 3 + [row(pool_w)] + [row(d)] * 2
        tile_bytes = _nbytes((row_tile, d), F32) * 6 + _nbytes((row_tile, pool_w), F32)
    return pl.pallas_call(
        functools.partial(_project_in_kernel, d_model=d, prompt_extras=prompt_extras, scale=scale),
        grid=(nseq, tiles),
        in_specs=[row(d), const((1, d)), const((d, in_w)), const((1, dh)), const((1, dh))],
        out_specs=out_specs,
        out_shape=out_shape,
        scratch_shapes=scratch,
        compiler_params=pltpu.CompilerParams(
            dimension_semantics=("arbitrary", "arbitrary"),
            vmem_limit_bytes=_vmem_limit(tile_bytes, 2 * _nbytes((d, in_w), BF16),
                                         6 * _nbytes((row_tile, d), F32))),
    )(x3, g1, w_bf16, qg, kg)


def _moba_prompt_kernel(pt_ref, qs_ref, bias_ref, kb_ref, vt_ref, vt_own_ref, ck_ref, o_ref, ksum_ref,
                        kaug_ref, s_ref, pbuf, psem, *, pages_per_step, total_pages, exact_fit):
    h = pl.program_id(1)
    i = pl.program_id(2)
    seq, dh = kb_ref.shape
    n_blocks = seq // MOBA_BLOCK
    pair = 2 * MOBA_BLOCK

    n_steps = pl.num_programs(0) * pl.num_programs(1) * pl.num_programs(2)
    step = (pl.program_id(0) * pl.num_programs(1) + h) * pl.num_programs(2) + i
    slot = step % 2

    def page_copy(of_step, k, in_slot):
        page = pt_ref[of_step * pages_per_step + k]
        return pltpu.make_async_copy(ck_ref.at[page], pbuf.at[in_slot, k], psem.at[in_slot, k])

    def for_pages(of_step, fn):
        for k in range(pages_per_step):
            if exact_fit:
                fn(k)
            else:
                pl.when(of_step * pages_per_step + k < total_pages)(functools.partial(fn, k))

    @pl.when(step == 0)
    def _():
        for_pages(0, lambda k: page_copy(0, k, 0).start())

    @pl.when(step + 1 < n_steps)
    def _():
        for_pages(step + 1, lambda k: page_copy(step + 1, k, 1 - slot).start())

    @pl.when(i == 0)
    def _():
        kaug_ref[:, :dh] = kb_ref[...]
        row_blk = lax.broadcasted_iota(jnp.int32, (seq, V7X_LANES), 0) // MOBA_BLOCK
        lane = lax.broadcasted_iota(jnp.int32, (seq, V7X_LANES), 1)
        kaug_ref[:, dh:] = jnp.where(lane == h * HEAD_BIAS_LANES + row_blk, 1.0, 0.0).astype(BF16)

    qs = qs_ref[...]
    q_aug = jnp.concatenate([qs, bias_ref[...]], axis=1)
    own = pl.multiple_of(i * MOBA_BLOCK, MOBA_BLOCK)

    def attend(n_past):
        chunks = [slice(c * pair, (c + 1) * pair) for c in range(n_past // 2)]
        own_rows = slice(n_past * MOBA_BLOCK, (n_past + 1) * MOBA_BLOCK)
        s = _dot_nt(kb_ref[pl.ds(own, MOBA_BLOCK), :], qs)
        key_i = lax.broadcasted_iota(jnp.int32, s.shape, 0)
        qry_i = lax.broadcasted_iota(jnp.int32, s.shape, 1)
        s = jnp.where(key_i <= qry_i, s, NEG_INF)
        s_ref[own_rows, :] = s
        m = jnp.max(s, axis=0, keepdims=True)
        for rows in chunks:
            s = _dot_nt(kaug_ref[rows, :], q_aug)
            s_ref[rows, :] = s
            m = jnp.maximum(m, jnp.max(s, axis=0, keepdims=True))
        p = jnp.exp2(s_ref[own_rows, :] - m)
        l = jnp.sum(p, axis=0, keepdims=True)
        acc = jnp.dot(vt_own_ref[...], p.astype(BF16), preferred_element_type=F32)
        for rows in chunks:
            p = jnp.exp2(s_ref[rows, :] - m)
            l = l + jnp.sum(p, axis=0, keepdims=True)
            acc = acc + jnp.dot(vt_ref[:, rows], p.astype(BF16), preferred_element_type=F32)
        o_ref[...] = (acc / l).T

    n_quads = (i + MOBA_QUAD - 1) // MOBA_QUAD
    for quads in range((n_blocks - 1 + MOBA_QUAD - 1) // MOBA_QUAD + 1):
        pl.when(n_quads == quads)(functools.partial(attend, min(quads * MOBA_QUAD, n_blocks)))

    for_pages(step, lambda k: page_copy(step, k, slot).wait())
    if not exact_fit:
        for k in range(pages_per_step):
            @pl.when(step * pages_per_step + k >= total_pages)
            def _():
                pbuf[slot, k] = jnp.zeros(pbuf.shape[2:], F32)
    n_out, n_heads, _ = ksum_ref.shape
    rows = pages_per_step * PAGE_SIZE // n_out
    pages = pbuf[slot].reshape(n_out, PARTIAL_SUMS, rows // PARTIAL_SUMS, n_heads, dh)
    ksum_ref[...] = jnp.sum(jnp.sum(pages, axis=2), axis=1)


def _moba_prompt(qs, bias, kb, vt, page_list, cache_k):
    b, s, d = qs.shape
    dh = d // N_HEADS
    assert s % (2 * MOBA_BLOCK) == 0 and dh == V7X_LANES
    n_blocks = s // MOBA_BLOCK
    assert n_blocks <= HEAD_BIAS_LANES
    tq = MOBA_BLOCK
    pages_per_block = MOBA_BLOCK // PAGE_SIZE
    total_pages = page_list.shape[0]
    assert total_pages % pages_per_block == 0
    n_steps = b * N_HEADS * n_blocks
    blocks_per_step = -(-(total_pages // pages_per_block) // n_steps)
    pages_per_step = blocks_per_step * pages_per_block
    page = cache_k.shape[1:]
    tile = pl.BlockSpec((None, tq, dh), lambda bi, h, i, pt: (bi, i, h))
    grid_spec = pltpu.PrefetchScalarGridSpec(
        num_scalar_prefetch=1,
        grid=(b, N_HEADS, n_blocks),
        in_specs=[tile,
                  pl.BlockSpec((None, tq, V7X_LANES), lambda bi, h, i, pt: (bi, i, 0)),
                  pl.BlockSpec((None, s, dh), lambda bi, h, i, pt: (bi, 0, h)),
                  pl.BlockSpec((None, dh, s), lambda bi, h, i, pt: (bi, h, 0)),
                  pl.BlockSpec((None, dh, tq), lambda bi, h, i, pt: (bi, h, i)),
                  pl.BlockSpec(memory_space=pl.ANY)],
        out_specs=[tile,
                   pl.BlockSpec((blocks_per_step,) + page[1:],
                                lambda bi, h, i, pt: ((bi * N_HEADS + h) * n_blocks + i, 0, 0))],
        scratch_shapes=[pltpu.VMEM((s, dh + V7X_LANES), BF16),
                        pltpu.VMEM((s + MOBA_BLOCK, tq), F32),
                        pltpu.VMEM((2, pages_per_step) + page, F32),
                        pltpu.SemaphoreType.DMA((2, pages_per_step))])
    return pl.pallas_call(
        functools.partial(_moba_prompt_kernel, pages_per_step=pages_per_step, total_pages=total_pages,
                          exact_fit=n_steps * pages_per_step == total_pages),
        grid_spec=grid_spec,
        out_shape=[jax.ShapeDtypeStruct((b, s, d), F32),
                   jax.ShapeDtypeStruct((n_steps * blocks_per_step,) + page[1:], F32)],
        compiler_params=pltpu.CompilerParams(
            dimension_semantics=("arbitrary", "arbitrary", "arbitrary"),
            vmem_limit_bytes=_vmem_limit(
                2 * _nbytes((s, dh), BF16) + 3 * _nbytes((tq, dh), F32)
                + _nbytes((blocks_per_step,) + page[1:], F32),
                _nbytes((s, dh + V7X_LANES), BF16) + _nbytes((s + MOBA_BLOCK, tq), F32)
                + _nbytes((2, pages_per_step) + page, F32),
                16 * _nbytes((2 * MOBA_BLOCK, tq), F32))),
    )(page_list, qs, bias, kb, vt, vt, cache_k)


def _sample_select_kernel(q_ref, ksum_ref, idx_ref):
    n_blocks, n_heads, dh = ksum_ref.shape
    t_new = q_ref.shape[0]
    kmean = ksum_ref[...] * (1.0 / MOBA_BLOCK)
    blk = lax.broadcasted_iota(jnp.int32, (n_blocks, n_heads, 1), 0).astype(F32)
    lane = lax.broadcasted_iota(jnp.int32, (n_heads, V7X_LANES), 1)
    for t in range(t_new):
        g = jnp.sum(kmean * q_ref[t][None], axis=-1, keepdims=True)
        tile = jnp.zeros((n_heads, V7X_LANES), F32)
        for r in range(MOBA_TOP_K):
            m = jnp.max(g, axis=0, keepdims=True)
            first = jnp.min(jnp.where(g == m, blk, float(n_blocks)), axis=0, keepdims=True)
            tile = jnp.where(lane == r, first[0], tile)
            g = jnp.where(blk == first, PICKED, g)
        idx_ref[t] = tile.astype(jnp.int32)


def _sample_select(q4, ksum4):
    db, t_new, n_heads, dh = q4.shape
    n_blocks = ksum4.shape[1]
    assert n_blocks >= MOBA_TOP_K and n_heads == V7X_SUBLANES and dh == V7X_LANES
    return pl.pallas_call(
        _sample_select_kernel,
        grid=(db,),
        in_specs=[pl.BlockSpec((None, t_new, n_heads, dh), lambda b: (b, 0, 0, 0)),
                  pl.BlockSpec((None, n_blocks, n_heads, dh), lambda b: (b, 0, 0, 0))],
        out_specs=pl.BlockSpec((None, t_new, n_heads, V7X_LANES), lambda b: (b, 0, 0, 0)),
        out_shape=jax.ShapeDtypeStruct((db, t_new, n_heads, V7X_LANES), jnp.int32),
        compiler_params=pltpu.CompilerParams(
            dimension_semantics=("arbitrary",),
            vmem_limit_bytes=_vmem_limit(
                _nbytes((n_blocks, n_heads, dh), F32) + 2 * _nbytes((t_new, n_heads, V7X_LANES), F32),
                0, 8 * _nbytes((n_blocks, n_heads, dh), F32))),
    )(q4, ksum4)


def _sample_attend_kernel(pt_ref, sel_ref, q_ref, kn_ref, vn_ref, ck_ref, cv_ref, o_ref,
                          kg, vg, sem, *, pages_per_block, scale):
    b = pl.program_id(0)
    t_new, d = q_ref.shape
    dh = d // N_HEADS
    rows_per_token = MOBA_TOP_K * MOBA_BLOCK

    def gather_copies(h, slot):
        copies = []
        for t in range(t_new):
            for r in range(MOBA_TOP_K):
                blk = sel_ref[b, (t * N_HEADS + h) * MOBA_TOP_K + r]
                for pp in range(pages_per_block):
                    phys = pt_ref[b, blk * pages_per_block + pp]
                    row0 = ((t * MOBA_TOP_K + r) * pages_per_block + pp) * PAGE_SIZE
                    dst = pl.ds(row0, PAGE_SIZE)
                    copies.append(pltpu.make_async_copy(
                        ck_ref.at[phys, :, h, :], kg.at[slot, dst, :], sem.at[0, slot]))
                    copies.append(pltpu.make_async_copy(
                        cv_ref.at[phys, :, h, :], vg.at[slot, dst, :], sem.at[1, slot]))
        return copies

    q = q_ref[...]
    kn = kn_ref[...]
    vn = vn_ref[...]
    own_row = lax.broadcasted_iota(jnp.int32, (t_new, 1), 0)

    pending = gather_copies(0, 0)
    for c in pending:
        c.start()
    for h in range(N_HEADS):
        slot = h % 2
        current = pending
        if h + 1 < N_HEADS:
            pending = gather_copies(h + 1, 1 - slot)
            for c in pending:
                c.start()
        for c in current:
            c.wait()
        sl = slice(h * dh, (h + 1) * dh)
        kn_h = kn[:, sl]
        vn_h = vn[:, sl]
        for t in range(t_new):
            rows = pl.ds(t * rows_per_token, rows_per_token)
            qrow = q[t:t + 1, sl] * scale
            s = jnp.sum(kg[slot, rows, :] * qrow, axis=-1, keepdims=True)
            s_own = jnp.sum(kn_h * qrow, axis=-1, keepdims=True)
            s_own = jnp.where(own_row <= t, s_own, NEG_INF)
            m = jnp.maximum(jnp.max(s, axis=0, keepdims=True), jnp.max(s_own, axis=0, keepdims=True))
            p = jnp.exp(s - m)
            p_own = jnp.exp(s_own - m)
            denom = jnp.sum(p, axis=0, keepdims=True) + jnp.sum(p_own, axis=0, keepdims=True)
            o = (jnp.sum(p * vg[slot, rows, :], axis=0, keepdims=True)
                 + jnp.sum(p_own * vn_h, axis=0, keepdims=True))
            o_ref[t:t + 1, sl] = o / denom


def _sample_attend(page_table, sel, q3, kn3, vn3, cache_k, cache_v, *, scale):
    db, t_new, d = q3.shape
    dh = d // N_HEADS
    pages_per_block = MOBA_BLOCK // PAGE_SIZE
    gathered = (2, t_new * MOBA_TOP_K * MOBA_BLOCK, dh)
    tok = pl.BlockSpec((None, t_new, d), lambda b, pt, sel: (b, 0, 0))
    grid_spec = pltpu.PrefetchScalarGridSpec(
        num_scalar_prefetch=2,
        grid=(db,),
        in_specs=[tok, tok, tok, pl.BlockSpec(memory_space=pl.ANY), pl.BlockSpec(memory_space=pl.ANY)],
        out_specs=tok,
        scratch_shapes=[pltpu.VMEM(gathered, F32), pltpu.VMEM(gathered, F32),
                        pltpu.SemaphoreType.DMA((2, 2))])
    return pl.pallas_call(
        functools.partial(_sample_attend_kernel, pages_per_block=pages_per_block, scale=scale),
        grid_spec=grid_spec,
        out_shape=jax.ShapeDtypeStruct((db, t_new, d), F32),
        compiler_params=pltpu.CompilerParams(
            dimension_semantics=("arbitrary",),
            vmem_limit_bytes=_vmem_limit(4 * _nbytes((V7X_SUBLANES, d), F32),
                                         2 * _nbytes(gathered, F32),
                                         8 * _nbytes((MOBA_TOP_K * MOBA_BLOCK, dh), F32))),
    )(page_table, sel, q3, kn3, vn3, cache_k, cache_v)


def _pool_mix_kernel(u_ref, halo_ref, gp_ref, ga_ref, attn_ref, wp_ref, ps_ref, o_ref, uext, *,
                     base_pos):
    t_rows, pool_w = u_ref.shape
    t_pad = uext.shape[0] - POOL_HALO
    gdim = pool_w // N_POOL_GROUPS
    gout = wp_ref.shape[-1]
    pos0 = base_pos + pl.program_id(1) * t_rows

    halo_pos = pos0 - POOL_HALO + lax.broadcasted_iota(jnp.int32, (POOL_HALO, 1), 0)
    uext[0:POOL_HALO, :] = jnp.where(halo_pos >= 0, halo_ref[...], 0.0)
    uext[POOL_HALO:POOL_HALO + t_rows, :] = u_ref[...]
    if t_pad > t_rows:
        uext[POOL_HALO + t_rows:, :] = jnp.zeros((t_pad - t_rows, pool_w), F32)

    pos = pos0 + lax.broadcasted_iota(jnp.int32, (t_pad, 1), 0)
    for g, window in enumerate(POOL_WINDOWS):
        cols = slice(g * gdim, (g + 1) * gdim)
        tok = uext[POOL_HALO:POOL_HALO + t_pad, cols]
        total = tok
        for r in range(1, window):
            total = total + uext[POOL_HALO - r:POOL_HALO - r + t_pad, cols]
        cnt = jnp.minimum(window, pos + 1).astype(F32)
        diff = total / cnt - tok
        y = jnp.dot(diff.astype(BF16), wp_ref[g], preferred_element_type=F32)[:t_rows]
        oc = slice(g * gout, (g + 1) * gout)
        pool = y * ps_ref[:, oc]
        o_ref[:, oc] = (jax.nn.sigmoid(gp_ref[:, oc]) * pool
                        + jax.nn.sigmoid(ga_ref[:, oc]) * attn_ref[:, oc]).astype(o_ref.dtype)


def _pool_mix(u3, halo3, halo_index, gp3, ga3, attn3, w_pool_bf16, pool_scale, *, row_tile,
              base_pos, out_dtype):
    nseq, s, pool_w = u3.shape
    d = gp3.shape[-1]
    assert s % row_tile == 0
    t_pad = -(-row_tile // V7X_SUBLANES) * V7X_SUBLANES
    wide = pl.BlockSpec((None, row_tile, d), lambda q, i: (q, i, 0))
    return pl.pallas_call(
        functools.partial(_pool_mix_kernel, base_pos=base_pos),
        grid=(nseq, s // row_tile),
        in_specs=[pl.BlockSpec((None, row_tile, pool_w), lambda q, i: (q, i, 0)),
                  pl.BlockSpec((None, POOL_HALO, pool_w), halo_index),
                  wide, wide, wide,
                  pl.BlockSpec(w_pool_bf16.shape, lambda q, i: (0, 0, 0)),
                  pl.BlockSpec((1, d), lambda q, i: (0, 0))],
        out_specs=wide,
        out_shape=jax.ShapeDtypeStruct((nseq, s, d), out_dtype),
        scratch_shapes=[pltpu.VMEM((POOL_HALO + t_pad, pool_w), F32)],
        compiler_params=pltpu.CompilerParams(
            dimension_semantics=("arbitrary", "arbitrary"),
            vmem_limit_bytes=_vmem_limit(5 * _nbytes((t_pad, d), F32),
                                         _nbytes(w_pool_bf16.shape, BF16),
                                         6 * _nbytes((t_pad, d), F32))),
    )(u3, halo3, gp3, ga3, attn3, w_pool_bf16, pool_scale)


def _out_mlp_kernel(x_ref, mixed_ref, wo_ref, g2_ref, wu_ref, wd_ref, y_ref):
    d_ff = wu_ref.shape[1]
    hres = x_ref[...] + jnp.dot(mixed_ref[...].astype(BF16), wo_ref[...], preferred_element_type=F32)
    hn = (hres * _rms_scale(hres) * g2_ref[...]).astype(BF16)
    acc = hres
    for c in range(d_ff // FF_CHUNK):
        cols = slice(c * FF_CHUNK, (c + 1) * FF_CHUNK)
        a = jnp.dot(hn, wu_ref[:, cols], preferred_element_type=F32)
        a = jnp.square(jnp.maximum(a, 0.0)).astype(BF16)
        acc = acc + jnp.dot(a, wd_ref[cols, :], preferred_element_type=F32)
    y_ref[...] = acc


def _out_mlp(x2d, mixed2d, w_out, g2, w_up, w_down, *, row_tile):
    n, d = x2d.shape
    d_ff = w_up.shape[1]
    assert n % row_tile == 0 and d_ff % FF_CHUNK == 0
    row = pl.BlockSpec((row_tile, d), lambda i: (i, 0))
    const = lambda shape: pl.BlockSpec(shape, lambda i: (0, 0))
    weights = _nbytes((d, d), BF16) + 2 * _nbytes((d, d_ff), BF16)
    return pl.pallas_call(
        _out_mlp_kernel,
        grid=(n // row_tile,),
        in_specs=[row, row, const((d, d)), const((1, d)), const((d, d_ff)), const((d_ff, d))],
        out_specs=row,
        out_shape=jax.ShapeDtypeStruct((n, d), F32),
        compiler_params=pltpu.CompilerParams(
            dimension_semantics=("arbitrary",),
            vmem_limit_bytes=_vmem_limit(3 * _nbytes((row_tile, d), F32), 2 * weights,
                                         6 * _nbytes((row_tile, FF_CHUNK), F32))),
    )(x2d, mixed2d, w_out, g2, w_up, w_down)


def kernel(x_prompt, x_sample, cache_k, cache_v, state_pool, page_table, norm1_g, w_in,
           q_norm_g, k_norm_g, w_pool, pool_scale, w_out, norm2_g, w_up, w_down):
    depth = w_in.shape[0]
    b, s, d = x_prompt.shape
    db, t_new, _ = x_sample.shape
    dh = d // N_HEADS
    n_pages = page_table.shape[1]
    past = n_pages * PAGE_SIZE
    scale = dh ** -0.5
    pool_buf = state_pool.shape[2]
    assert pool_buf == POOL_HALO - 1
    assert n_pages % (MOBA_BLOCK // PAGE_SIZE) == 0, "cached rows must fill whole MoBA blocks"
    n_cached = n_pages // (MOBA_BLOCK // PAGE_SIZE)

    xp = x_prompt.reshape(b * s, d)
    xs = x_sample.reshape(db * t_new, d)
    outs = [[] for _ in range(6)]
    for l in range(depth):
        w_in_l = w_in[l].astype(BF16)
        w_pool_l = w_pool[l].astype(BF16)
        w_out_l = w_out[l].astype(BF16)
        w_up_l = w_up[l].astype(BF16)
        w_down_l = w_down[l].astype(BF16)
        g1, g2 = norm1_g[l][None], norm2_g[l][None]
        qg, kg = q_norm_g[l][None], k_norm_g[l][None]
        ps = pool_scale[l][None]

        shape3 = lambda a: a.reshape(b, s, a.shape[-1])
        k, v, u, gp, ga, qs, bias, kb, vt = _project_in(
            shape3(xp), g1, w_in_l, qg, kg, row_tile=MOBA_BLOCK, prompt_extras=True, scale=scale)
        attn, cache_ksum = _moba_prompt(qs, bias, kb, vt, page_table.reshape(-1), cache_k[l])
        tile = MOBA_BLOCK
        halo_blocks = tile // POOL_HALO
        mixed = _pool_mix(
            u, u, lambda qi, i: (qi, jnp.maximum(i * halo_blocks - 1, 0), 0),
            gp, ga, attn, w_pool_l, ps, row_tile=tile, base_pos=0, out_dtype=BF16)
        xp = _out_mlp(xp, mixed.reshape(b * s, d), w_out_l, g2, w_up_l, w_down_l, row_tile=tile)
        outs[0].append(k.reshape(b, s, N_HEADS, dh))
        outs[1].append(v.reshape(b, s, N_HEADS, dh))
        outs[2].append(u[:, s - pool_buf:])

        q, k, v, u, gp, ga = _project_in(
            xs[None], g1, w_in_l, qg, kg, row_tile=db * t_new, prompt_extras=False, scale=scale)
        tok3 = lambda a: a.reshape(db, t_new, a.shape[-1])
        picked = _sample_select(q.reshape(db, t_new, N_HEADS, dh),
                                cache_ksum[:db * n_cached].reshape(db, n_cached, N_HEADS, dh))
        sel = picked[:, :, :, :MOBA_TOP_K].reshape(db, t_new * N_HEADS * MOBA_TOP_K)
        attn = _sample_attend(page_table, sel, tok3(q), tok3(k), tok3(v), cache_k[l], cache_v[l],
                              scale=scale)
        state = state_pool[l].astype(F32)
        halo = jnp.pad(state, ((0, 0), (POOL_HALO - pool_buf, 0), (0, 0)))
        mixed = _pool_mix(
            tok3(u), halo, lambda qi, i: (qi, 0, 0), tok3(gp), tok3(ga), attn, w_pool_l, ps,
            row_tile=t_new, base_pos=past, out_dtype=F32)
        xs = _out_mlp(xs, mixed.reshape(db * t_new, d), w_out_l, g2, w_up_l, w_down_l,
                      row_tile=db * t_new)
        outs[3].append(k.reshape(db, t_new, N_HEADS, dh))
        outs[4].append(v.reshape(db, t_new, N_HEADS, dh))
        outs[5].append(jnp.concatenate([state, tok3(u)], axis=1)[:, -pool_buf:])

    stacked = [jnp.stack(o, axis=0) for o in outs]
    return (xp.reshape(b, s, d), xs.reshape(db, t_new, d), *stacked)
```

```python
import functools

import jax
import jax.numpy as jnp
from jax import lax
from jax.experimental import pallas as pl
from jax.experimental.pallas import tpu as pltpu

PAGE_SIZE = 128
N_HEADS = 8
MOBA_BLOCK = 256
MOBA_TOP_K = 3
POOL_WINDOWS = (2, 4, 8, 16)
N_POOL_GROUPS = len(POOL_WINDOWS)
EPS = 1e-6
NEG_INF = -1e30
PICKED = -3e38
LOG2_E = 1.4426950408889634

V7X_LANES = 128
V7X_SUBLANES = 8
V7X_VMEM_BYTES = 64 * 1024 * 1024

POOL_HALO = 16
FF_CHUNK = 1024
HEAD_BIAS_LANES = V7X_LANES // N_HEADS
MOBA_QUAD = 4
MOBA_HEADS_PER_STEP = 2
PARTIAL_SUMS = 8
GATHER_SLOTS = 4
GATHER_AHEAD = 3

F32 = jnp.float32
BF16 = jnp.bfloat16


def _vmem_limit(pipelined_bytes, resident_bytes, live_bytes):
    est = 2 * pipelined_bytes + resident_bytes + live_bytes
    return int(min(max(est, 16 * 1024 * 1024), V7X_VMEM_BYTES - 8 * 1024 * 1024))


def _nbytes(shape, dtype):
    n = 1
    for s in shape:
        n *= s
    return n * jnp.dtype(dtype).itemsize


def _rms_scale(x):
    return lax.rsqrt(jnp.mean(x * x, axis=-1, keepdims=True) + EPS)


def _dot_nt(a, b, precision=None):
    return lax.dot_general(a, b, (((1,), (1,)), ((), ())), precision=precision,
                           preferred_element_type=F32)


def _dot_nt_3pass(a, b):
    a_hi = a.astype(BF16)
    b_hi = b.astype(BF16)
    a_lo = (a - a_hi.astype(F32)).astype(BF16)
    b_lo = (b - b_hi.astype(F32)).astype(BF16)
    return _dot_nt(a_hi, b_hi) + (_dot_nt(a_hi, b_lo) + _dot_nt(a_lo, b_hi))


def _project_in_kernel(x_ref, g1_ref, w_ref, qg_ref, kg_ref, *refs, d_model, prompt_extras, scale):
    if prompt_extras:
        k_ref, v_ref, u_ref, gp_ref, ga_ref, qs_ref, bias_ref, kb_ref, vt_ref, kmean_ref = refs
    else:
        q_ref, k_ref, v_ref, u_ref, gp_ref, ga_ref = refs
    dh = d_model // N_HEADS
    pool_w = u_ref.shape[-1]
    rows = x_ref.shape[0]
    o1, o2, o3 = d_model, 2 * d_model, 3 * d_model
    o4 = o3 + pool_w
    o5 = o4 + d_model
    o6 = o5 + d_model

    x = x_ref[...]
    xn = (x * _rms_scale(x) * g1_ref[...]).astype(BF16)

    def proj(c0, c1):
        return jnp.dot(xn, w_ref[:, c0:c1], preferred_element_type=F32)

    if prompt_extras:
        i = pl.program_id(1)

        @pl.when(jnp.logical_and(pl.program_id(0) == 0, i == 0))
        def _():
            kmean_ref[...] = jnp.zeros(kmean_ref.shape, F32)

    zq = proj(0, o1)
    gates = []
    for h in range(N_HEADS):
        sl = slice(h * dh, (h + 1) * dh)
        zh = zq[:, sl]
        qh = zh * _rms_scale(zh) * qg_ref[...]
        if prompt_extras:
            qs_ref[:, sl] = (qh * (scale * LOG2_E)).astype(BF16)
            gates.append(_dot_nt_3pass(kmean_ref[:, sl], qh))
        else:
            q_ref[:, sl] = qh

    if prompt_extras:
        g = jnp.stack(gates, axis=0)
        blk = lax.broadcasted_iota(jnp.int32, g.shape, 1)
        past = blk < i
        g = jnp.where(past, g, NEG_INF)
        sel = jnp.zeros(g.shape, jnp.bool_)
        for _ in range(MOBA_TOP_K):
            m = jnp.max(g, axis=1, keepdims=True)
            first = jnp.min(jnp.where(g == m, blk, HEAD_BIAS_LANES), axis=1, keepdims=True)
            pick = blk == first
            sel = jnp.logical_or(sel, pick)
            g = jnp.where(pick, PICKED, g)
        bias_t = jnp.where(jnp.logical_and(sel, past), 0.0, NEG_INF)
        bias_ref[...] = bias_t.reshape(V7X_LANES, rows).T.astype(BF16)

    zk = proj(o1, o2)
    for h in range(N_HEADS):
        sl = slice(h * dh, (h + 1) * dh)
        zh = zk[:, sl]
        kh = zh * _rms_scale(zh) * kg_ref[...]
        k_ref[:, sl] = kh
        if prompt_extras:
            kb_ref[:, sl] = kh.astype(BF16)
            block_row = lax.broadcasted_iota(jnp.int32, (HEAD_BIAS_LANES, dh), 0) == i
            kmean_ref[:, sl] = jnp.where(
                block_row, jnp.sum(kh, axis=0, keepdims=True) * (1.0 / MOBA_BLOCK), kmean_ref[:, sl])
    zv = proj(o2, o3)
    v_ref[...] = zv
    if prompt_extras:
        vt_ref[...] = zv.T.astype(BF16)
    u_ref[...] = proj(o3, o4)
    gp_ref[...] = proj(o4, o5)
    ga_ref[...] = proj(o5, o6)


def _project_in(x3, g1, w_bf16, qg, kg, *, row_tile, prompt_extras, scale):
    nseq, s, d = x3.shape
    in_w = w_bf16.shape[1]
    pool_w = in_w - 5 * d
    dh = d // N_HEADS
    assert s % row_tile == 0
    tiles = s // row_tile
    row = lambda width: pl.BlockSpec((None, row_tile, width), lambda q, i: (q, i, 0))
    const = lambda shape: pl.BlockSpec(shape, lambda q, i: (0, 0))
    wide = jax.ShapeDtypeStruct((nseq, s, d), F32)
    scratch = []
    if prompt_extras:
        assert row_tile == MOBA_BLOCK and tiles <= HEAD_BIAS_LANES and dh == V7X_LANES
        out_shape = [wide, wide, jax.ShapeDtypeStruct((nseq, s, pool_w), F32), wide, wide,
                     jax.ShapeDtypeStruct((nseq, s, d), BF16),
                     jax.ShapeDtypeStruct((nseq, s, V7X_LANES), BF16),
                     jax.ShapeDtypeStruct((nseq, s, d), BF16),
                     jax.ShapeDtypeStruct((nseq, d, s), BF16)]
        out_specs = [row(d), row(d), row(pool_w), row(d), row(d), row(d), row(V7X_LANES), row(d),
                     pl.BlockSpec((None, d, row_tile), lambda q, i: (q, 0, i))]
        tile_bytes = (_nbytes((row_tile, d), F32) * 5 + _nbytes((row_tile, pool_w), F32)
                      + 4 * _nbytes((row_tile, d), BF16))
        scratch = [pltpu.VMEM((HEAD_BIAS_LANES, d), F32)]
    else:
        out_shape = [wide] * 3 + [jax.ShapeDtypeStruct((nseq, s, pool_w), F32)] + [wide] * 2
        out_specs = [row(d)] * 3 + [row(pool_w)] + [row(d)] * 2
        tile_bytes = _nbytes((row_tile, d), F32) * 6 + _nbytes((row_tile, pool_w), F32)
    return pl.pallas_call(
        functools.partial(_project_in_kernel, d_model=d, prompt_extras=prompt_extras, scale=scale),
        grid=(nseq, tiles),
        in_specs=[row(d), const((1, d)), const((d, in_w)), const((1, dh)), const((1, dh))],
        out_specs=out_specs,
        out_shape=out_shape,
        scratch_shapes=scratch,
        compiler_params=pltpu.CompilerParams(
            dimension_semantics=("arbitrary", "arbitrary"),
            vmem_limit_bytes=_vmem_limit(tile_bytes, 2 * _nbytes((d, in_w), BF16),
                                         6 * _nbytes((row_tile, d), F32))),
    )(x3, g1, w_bf16, qg, kg)


def _moba_prompt_kernel(pt_ref, qs_ref, bias_ref, kb_ref, vt_ref, vt_own_ref, ck_ref, o_ref, ksum_ref,
                        kaug_ref, s_ref, pbuf, psem, *, pages_per_step, total_pages, exact_fit):
    hg = pl.program_id(1)
    i = pl.program_id(2)
    seq = kb_ref.shape[0]
    n_heads = kaug_ref.shape[0]
    dh = kb_ref.shape[1] // n_heads
    n_blocks = seq // MOBA_BLOCK
    pair = 2 * MOBA_BLOCK

    n_steps = pl.num_programs(0) * pl.num_programs(1) * pl.num_programs(2)
    step = (pl.program_id(0) * pl.num_programs(1) + hg) * pl.num_programs(2) + i
    slot = step % 2

    def page_copy(of_step, k, in_slot):
        page = pt_ref[of_step * pages_per_step + k]
        return pltpu.make_async_copy(ck_ref.at[page], pbuf.at[in_slot, k], psem.at[in_slot, k])

    def for_pages(of_step, fn):
        for k in range(pages_per_step):
            if exact_fit:
                fn(k)
            else:
                pl.when(of_step * pages_per_step + k < total_pages)(functools.partial(fn, k))

    @pl.when(step == 0)
    def _():
        for_pages(0, lambda k: page_copy(0, k, 0).start())

    @pl.when(step + 1 < n_steps)
    def _():
        for_pages(step + 1, lambda k: page_copy(step + 1, k, 1 - slot).start())

    @pl.when(i == 0)
    def _():
        row_blk = lax.broadcasted_iota(jnp.int32, (seq, V7X_LANES), 0) // MOBA_BLOCK
        lane = lax.broadcasted_iota(jnp.int32, (seq, V7X_LANES), 1)
        for hh in range(n_heads):
            head = hg * n_heads + hh
            kaug_ref[hh, :, :dh] = kb_ref[:, hh * dh:(hh + 1) * dh]
            kaug_ref[hh, :, dh:] = jnp.where(lane == head * HEAD_BIAS_LANES + row_blk,
                                             1.0, 0.0).astype(BF16)

    own = pl.multiple_of(i * MOBA_BLOCK, MOBA_BLOCK)
    bias = bias_ref[...]
    heads = range(n_heads)
    cols = [slice(hh * dh, (hh + 1) * dh) for hh in heads]
    qs = [qs_ref[:, cols[hh]] for hh in heads]
    q_aug = [jnp.concatenate([qs[hh], bias], axis=1) for hh in heads]

    def attend(n_past):
        chunks = [slice(c * pair, (c + 1) * pair) for c in range(n_past // 2)]
        own_rows = slice(n_past * MOBA_BLOCK, (n_past + 1) * MOBA_BLOCK)
        m = []
        for hh in heads:
            s = _dot_nt(kb_ref[pl.ds(own, MOBA_BLOCK), cols[hh]], qs[hh])
            key_i = lax.broadcasted_iota(jnp.int32, s.shape, 0)
            qry_i = lax.broadcasted_iota(jnp.int32, s.shape, 1)
            s = jnp.where(key_i <= qry_i, s, NEG_INF)
            s_ref[hh, own_rows, :] = s
            m.append(jnp.max(s, axis=0, keepdims=True))
        for rows in chunks:
            for hh in heads:
                s = _dot_nt(kaug_ref[hh, rows, :], q_aug[hh])
                s_ref[hh, rows, :] = s
                m[hh] = jnp.maximum(m[hh], jnp.max(s, axis=0, keepdims=True))
        l, acc = [], []
        for hh in heads:
            p = jnp.exp2(s_ref[hh, own_rows, :] - m[hh])
            l.append(jnp.sum(p, axis=0, keepdims=True))
            acc.append(jnp.dot(vt_own_ref[cols[hh], :], p.astype(BF16),
                               preferred_element_type=F32))
        for rows in chunks:
            for hh in heads:
                p = jnp.exp2(s_ref[hh, rows, :] - m[hh])
                l[hh] = l[hh] + jnp.sum(p, axis=0, keepdims=True)
                acc[hh] = acc[hh] + jnp.dot(vt_ref[cols[hh], rows], p.astype(BF16),
                                            preferred_element_type=F32)
        for hh in heads:
            o_ref[:, cols[hh]] = (acc[hh] / l[hh]).T

    n_quads = (i + MOBA_QUAD - 1) // MOBA_QUAD
    for quads in range((n_blocks - 1 + MOBA_QUAD - 1) // MOBA_QUAD + 1):
        pl.when(n_quads == quads)(functools.partial(attend, min(quads * MOBA_QUAD, n_blocks)))

    for_pages(step, lambda k: page_copy(step, k, slot).wait())
    if not exact_fit:
        for k in range(pages_per_step):
            @pl.when(step * pages_per_step + k >= total_pages)
            def _():
                pbuf[slot, k] = jnp.zeros(pbuf.shape[2:], F32)
    n_out, page_heads, _ = ksum_ref.shape
    rows = pages_per_step * PAGE_SIZE // n_out
    pages = pbuf[slot].reshape(n_out, PARTIAL_SUMS, rows // PARTIAL_SUMS, page_heads, dh)
    ksum_ref[...] = jnp.sum(jnp.sum(pages, axis=2), axis=1)


def _moba_prompt(qs, bias, kb, vt, page_list, cache_k):
    b, s, d = qs.shape
    dh = d // N_HEADS
    assert s % (2 * MOBA_BLOCK) == 0 and dh == V7X_LANES
    n_blocks = s // MOBA_BLOCK
    assert n_blocks <= HEAD_BIAS_LANES
    tq = MOBA_BLOCK
    hps = MOBA_HEADS_PER_STEP
    hw = hps * dh
    pages_per_block = MOBA_BLOCK // PAGE_SIZE
    total_pages = page_list.shape[0]
    assert total_pages % pages_per_block == 0
    n_steps = b * (N_HEADS // hps) * n_blocks
    blocks_per_step = -(-(total_pages // pages_per_block) // n_steps)
    pages_per_step = blocks_per_step * pages_per_block
    page = cache_k.shape[1:]
    tile = pl.BlockSpec((None, tq, hw), lambda bi, h, i, pt: (bi, i, h))
    grid_spec = pltpu.PrefetchScalarGridSpec(
        num_scalar_prefetch=1,
        grid=(b, N_HEADS // hps, n_blocks),
        in_specs=[tile,
                  pl.BlockSpec((None, tq, V7X_LANES), lambda bi, h, i, pt: (bi, i, 0)),
                  pl.BlockSpec((None, s, hw), lambda bi, h, i, pt: (bi, 0, h)),
                  pl.BlockSpec((None, hw, s), lambda bi, h, i, pt: (bi, h, 0)),
                  pl.BlockSpec((None, hw, tq), lambda bi, h, i, pt: (bi, h, i)),
                  pl.BlockSpec(memory_space=pl.ANY)],
        out_specs=[tile,
                   pl.BlockSpec((blocks_per_step,) + page[1:],
                                lambda bi, h, i, pt: ((bi * (N_HEADS // hps) + h) * n_blocks + i, 0, 0))],
        scratch_shapes=[pltpu.VMEM((hps, s, dh + V7X_LANES), BF16),
                        pltpu.VMEM((hps, s + MOBA_BLOCK, tq), F32),
                        pltpu.VMEM((2, pages_per_step) + page, F32),
                        pltpu.SemaphoreType.DMA((2, pages_per_step))])
    return pl.pallas_call(
        functools.partial(_moba_prompt_kernel, pages_per_step=pages_per_step, total_pages=total_pages,
                          exact_fit=n_steps * pages_per_step == total_pages),
        grid_spec=grid_spec,
        out_shape=[jax.ShapeDtypeStruct((b, s, d), F32),
                   jax.ShapeDtypeStruct((n_steps * blocks_per_step,) + page[1:], F32)],
        compiler_params=pltpu.CompilerParams(
            dimension_semantics=("arbitrary", "arbitrary", "arbitrary"),
            vmem_limit_bytes=_vmem_limit(
                2 * _nbytes((s, hw), BF16) + 3 * _nbytes((tq, hw), F32)
                + _nbytes((blocks_per_step,) + page[1:], F32),
                _nbytes((hps, s, dh + V7X_LANES), BF16) + _nbytes((hps, s + MOBA_BLOCK, tq), F32)
                + _nbytes((2, pages_per_step) + page, F32),
                16 * hps * _nbytes((2 * MOBA_BLOCK, tq), F32))),
    )(page_list, qs, bias, kb, vt, vt, cache_k)


def _sample_select_kernel(q_ref, ksum_ref, idx_ref):
    n_blocks, n_heads, dh = ksum_ref.shape
    t_new = q_ref.shape[0]
    kmean = ksum_ref[...] * (1.0 / MOBA_BLOCK)
    blk = lax.broadcasted_iota(jnp.int32, (n_blocks, n_heads, 1), 0).astype(F32)
    lane = lax.broadcasted_iota(jnp.int32, (n_heads, V7X_LANES), 1)
    for t in range(t_new):
        g = jnp.sum(kmean * q_ref[t][None], axis=-1, keepdims=True)
        tile = jnp.zeros((n_heads, V7X_LANES), F32)
        for r in range(MOBA_TOP_K):
            m = jnp.max(g, axis=0, keepdims=True)
            first = jnp.min(jnp.where(g == m, blk, float(n_blocks)), axis=0, keepdims=True)
            tile = jnp.where(lane == r, first[0], tile)
            g = jnp.where(blk == first, PICKED, g)
        idx_ref[t] = tile.astype(jnp.int32)


def _sample_select(q4, ksum4):
    db, t_new, n_heads, dh = q4.shape
    n_blocks = ksum4.shape[1]
    assert n_blocks >= MOBA_TOP_K and n_heads == V7X_SUBLANES and dh == V7X_LANES
    return pl.pallas_call(
        _sample_select_kernel,
        grid=(db,),
        in_specs=[pl.BlockSpec((None, t_new, n_heads, dh), lambda b: (b, 0, 0, 0)),
                  pl.BlockSpec((None, n_blocks, n_heads, dh), lambda b: (b, 0, 0, 0))],
        out_specs=pl.BlockSpec((None, t_new, n_heads, V7X_LANES), lambda b: (b, 0, 0, 0)),
        out_shape=jax.ShapeDtypeStruct((db, t_new, n_heads, V7X_LANES), jnp.int32),
        compiler_params=pltpu.CompilerParams(
            dimension_semantics=("arbitrary",),
            vmem_limit_bytes=_vmem_limit(
                _nbytes((n_blocks, n_heads, dh), F32) + 2 * _nbytes((t_new, n_heads, V7X_LANES), F32),
                0, 8 * _nbytes((n_blocks, n_heads, dh), F32))),
    )(q4, ksum4)


def _sample_attend_kernel(pt_ref, sel_ref, q_ref, kn_ref, vn_ref, ck_ref, cv_ref, o_ref,
                          kg, vg, sem, *, pages_per_block, scale):
    b = pl.program_id(0)
    n_seqs = pl.num_programs(0)
    t_new, d = q_ref.shape
    dh = d // N_HEADS
    rows_per_token = MOBA_TOP_K * MOBA_BLOCK

    def unit_copies(seq, h):
        slot = h % GATHER_SLOTS
        copies = []
        for t in range(t_new):
            for r in range(MOBA_TOP_K):
                blk = sel_ref[seq, (t * N_HEADS + h) * MOBA_TOP_K + r]
                for pp in range(pages_per_block):
                    phys = pt_ref[seq, blk * pages_per_block + pp]
                    row0 = ((t * MOBA_TOP_K + r) * pages_per_block + pp) * PAGE_SIZE
                    dst = pl.ds(row0, PAGE_SIZE)
                    copies.append(pltpu.make_async_copy(
                        ck_ref.at[phys, :, h, :], kg.at[slot, dst, :], sem.at[0, slot]))
                    copies.append(pltpu.make_async_copy(
                        cv_ref.at[phys, :, h, :], vg.at[slot, dst, :], sem.at[1, slot]))
        return copies

    def start_unit(seq, h):
        for c in unit_copies(seq, h):
            c.start()

    @pl.when(b == 0)
    def _():
        for h in range(GATHER_AHEAD):
            start_unit(0, h)

    q = q_ref[...]
    kn = kn_ref[...]
    vn = vn_ref[...]
    own_row = lax.broadcasted_iota(jnp.int32, (t_new, 1), 0)

    for h in range(N_HEADS):
        ahead = h + GATHER_AHEAD
        if ahead < N_HEADS:
            start_unit(b, ahead)
        else:
            pl.when(b + 1 < n_seqs)(functools.partial(start_unit, b + 1, ahead - N_HEADS))
        for c in unit_copies(b, h):
            c.wait()
        slot = h % GATHER_SLOTS
        sl = slice(h * dh, (h + 1) * dh)
        kn_h = kn[:, sl]
        vn_h = vn[:, sl]
        for t in range(t_new):
            rows = pl.ds(t * rows_per_token, rows_per_token)
            qrow = q[t:t + 1, sl] * scale
            s = jnp.sum(kg[slot, rows, :] * qrow, axis=-1, keepdims=True)
            s_own = jnp.sum(kn_h * qrow, axis=-1, keepdims=True)
            s_own = jnp.where(own_row <= t, s_own, NEG_INF)
            m = jnp.maximum(jnp.max(s, axis=0, keepdims=True), jnp.max(s_own, axis=0, keepdims=True))
            p = jnp.exp(s - m)
            p_own = jnp.exp(s_own - m)
            denom = jnp.sum(p, axis=0, keepdims=True) + jnp.sum(p_own, axis=0, keepdims=True)
            o = (jnp.sum(p * vg[slot, rows, :], axis=0, keepdims=True)
                 + jnp.sum(p_own * vn_h, axis=0, keepdims=True))
            o_ref[t:t + 1, sl] = o / denom


def _sample_attend(page_table, sel, q3, kn3, vn3, cache_k, cache_v, *, scale):
    db, t_new, d = q3.shape
    dh = d // N_HEADS
    pages_per_block = MOBA_BLOCK // PAGE_SIZE
    assert N_HEADS % GATHER_SLOTS == 0 and GATHER_AHEAD < GATHER_SLOTS
    gathered = (GATHER_SLOTS, t_new * MOBA_TOP_K * MOBA_BLOCK, dh)
    tok = pl.BlockSpec((None, t_new, d), lambda b, pt, sel: (b, 0, 0))
    grid_spec = pltpu.PrefetchScalarGridSpec(
        num_scalar_prefetch=2,
        grid=(db,),
        in_specs=[tok, tok, tok, pl.BlockSpec(memory_space=pl.ANY), pl.BlockSpec(memory_space=pl.ANY)],
        out_specs=tok,
        scratch_shapes=[pltpu.VMEM(gathered, F32), pltpu.VMEM(gathered, F32),
                        pltpu.SemaphoreType.DMA((2, GATHER_SLOTS))])
    return pl.pallas_call(
        functools.partial(_sample_attend_kernel, pages_per_block=pages_per_block, scale=scale),
        grid_spec=grid_spec,
        out_shape=jax.ShapeDtypeStruct((db, t_new, d), F32),
        compiler_params=pltpu.CompilerParams(
            dimension_semantics=("arbitrary",),
            vmem_limit_bytes=_vmem_limit(4 * _nbytes((V7X_SUBLANES, d), F32),
                                         2 * _nbytes(gathered, F32),
                                         8 * _nbytes((MOBA_TOP_K * MOBA_BLOCK, dh), F32))),
    )(page_table, sel, q3, kn3, vn3, cache_k, cache_v)


def _pool_mix_tile(u_ref, halo_ref, gp_ref, ga_ref, attn_ref, wp_ref, ps_ref, uext, pos0):
    t_rows, pool_w = u_ref.shape
    t_pad = uext.shape[0] - POOL_HALO
    gdim = pool_w // N_POOL_GROUPS
    gout = wp_ref.shape[-1]

    halo_pos = pos0 - POOL_HALO + lax.broadcasted_iota(jnp.int32, (POOL_HALO, 1), 0)
    uext[0:POOL_HALO, :] = jnp.where(halo_pos >= 0, halo_ref[...], 0.0)
    uext[POOL_HALO:POOL_HALO + t_rows, :] = u_ref[...]
    if t_pad > t_rows:
        uext[POOL_HALO + t_rows:, :] = jnp.zeros((t_pad - t_rows, pool_w), F32)

    pos = pos0 + lax.broadcasted_iota(jnp.int32, (t_pad, 1), 0)
    slabs = []
    for g, window in enumerate(POOL_WINDOWS):
        cols = slice(g * gdim, (g + 1) * gdim)
        tok = uext[POOL_HALO:POOL_HALO + t_pad, cols]
        total = tok
        for r in range(1, window):
            total = total + uext[POOL_HALO - r:POOL_HALO - r + t_pad, cols]
        cnt = jnp.minimum(window, pos + 1).astype(F32)
        diff = total / cnt - tok
        y = jnp.dot(diff.astype(BF16), wp_ref[g], preferred_element_type=F32)[:t_rows]
        oc = slice(g * gout, (g + 1) * gout)
        pool = y * ps_ref[:, oc]
        slabs.append(jax.nn.sigmoid(gp_ref[:, oc]) * pool
                     + jax.nn.sigmoid(ga_ref[:, oc]) * attn_ref[:, oc])
    return slabs


def _mlp_tile(x, mixed, wo_ref, g2_ref, wu_ref, wd_ref):
    d_ff = wu_ref.shape[1]
    hres = x + jnp.dot(mixed.astype(BF16), wo_ref[...], preferred_element_type=F32)
    hn = (hres * _rms_scale(hres) * g2_ref[...]).astype(BF16)
    acc = hres
    for c in range(d_ff // FF_CHUNK):
        cols = slice(c * FF_CHUNK, (c + 1) * FF_CHUNK)
        a = jnp.dot(hn, wu_ref[:, cols], preferred_element_type=F32)
        a = jnp.square(jnp.maximum(a, 0.0)).astype(BF16)
        acc = acc + jnp.dot(a, wd_ref[cols, :], preferred_element_type=F32)
    return acc


def _pool_mix_kernel(u_ref, halo_ref, gp_ref, ga_ref, attn_ref, wp_ref, ps_ref, o_ref, uext, *,
                     base_pos):
    pos0 = base_pos + pl.program_id(1) * u_ref.shape[0]
    slabs = _pool_mix_tile(u_ref, halo_ref, gp_ref, ga_ref, attn_ref, wp_ref, ps_ref, uext, pos0)
    gout = wp_ref.shape[-1]
    for g, slab in enumerate(slabs):
        o_ref[:, g * gout:(g + 1) * gout] = slab


def _out_mlp_kernel(x_ref, mixed_ref, wo_ref, g2_ref, wu_ref, wd_ref, y_ref):
    y_ref[...] = _mlp_tile(x_ref[...], mixed_ref[...], wo_ref, g2_ref, wu_ref, wd_ref)


def _mix_mlp_kernel(x_ref, u_ref, halo_ref, gp_ref, ga_ref, attn_ref, wp_ref, ps_ref,
                    wo_ref, g2_ref, wu_ref, wd_ref, y_ref, uext, *, base_pos):
    pos0 = base_pos + pl.program_id(1) * u_ref.shape[0]
    slabs = _pool_mix_tile(u_ref, halo_ref, gp_ref, ga_ref, attn_ref, wp_ref, ps_ref, uext, pos0)
    mixed = jnp.concatenate(slabs, axis=1)
    y_ref[...] = _mlp_tile(x_ref[...], mixed, wo_ref, g2_ref, wu_ref, wd_ref)


def _pool_specs(row_tile, pool_w, d, halo_index, w_pool_shape):
    wide = pl.BlockSpec((None, row_tile, d), lambda q, i: (q, i, 0))
    return [pl.BlockSpec((None, row_tile, pool_w), lambda q, i: (q, i, 0)),
            pl.BlockSpec((None, POOL_HALO, pool_w), halo_index),
            wide, wide, wide,
            pl.BlockSpec(w_pool_shape, lambda q, i: (0, 0, 0)),
            pl.BlockSpec((1, d), lambda q, i: (0, 0))]


def _pool_mix(u3, halo3, halo_index, gp3, ga3, attn3, w_pool_bf16, pool_scale, *, row_tile, base_pos):
    nseq, s, pool_w = u3.shape
    d = gp3.shape[-1]
    assert s % row_tile == 0
    t_pad = -(-row_tile // V7X_SUBLANES) * V7X_SUBLANES
    return pl.pallas_call(
        functools.partial(_pool_mix_kernel, base_pos=base_pos),
        grid=(nseq, s // row_tile),
        in_specs=_pool_specs(row_tile, pool_w, d, halo_index, w_pool_bf16.shape),
        out_specs=pl.BlockSpec((None, row_tile, d), lambda q, i: (q, i, 0)),
        out_shape=jax.ShapeDtypeStruct((nseq, s, d), F32),
        scratch_shapes=[pltpu.VMEM((POOL_HALO + t_pad, pool_w), F32)],
        compiler_params=pltpu.CompilerParams(
            dimension_semantics=("arbitrary", "arbitrary"),
            vmem_limit_bytes=_vmem_limit(5 * _nbytes((t_pad, d), F32),
                                         _nbytes(w_pool_bf16.shape, BF16),
                                         6 * _nbytes((t_pad, d), F32))),
    )(u3, halo3, gp3, ga3, attn3, w_pool_bf16, pool_scale)


def _mlp_weight_specs(d, d_ff, index):
    return [pl.BlockSpec((d, d), index), pl.BlockSpec((1, d), index),
            pl.BlockSpec((d, d_ff), index), pl.BlockSpec((d_ff, d), index)]


def _out_mlp(x2d, mixed2d, w_out, g2, w_up, w_down, *, row_tile):
    n, d = x2d.shape
    d_ff = w_up.shape[1]
    assert n % row_tile == 0 and d_ff % FF_CHUNK == 0
    row = pl.BlockSpec((row_tile, d), lambda i: (i, 0))
    weights = _nbytes((d, d), BF16) + 2 * _nbytes((d, d_ff), BF16)
    return pl.pallas_call(
        _out_mlp_kernel,
        grid=(n // row_tile,),
        in_specs=[row, row] + _mlp_weight_specs(d, d_ff, lambda i: (0, 0)),
        out_specs=row,
        out_shape=jax.ShapeDtypeStruct((n, d), F32),
        compiler_params=pltpu.CompilerParams(
            dimension_semantics=("arbitrary",),
            vmem_limit_bytes=_vmem_limit(3 * _nbytes((row_tile, d), F32), 2 * weights,
                                         6 * _nbytes((row_tile, FF_CHUNK), F32))),
    )(x2d, mixed2d, w_out, g2, w_up, w_down)


def _mix_mlp(x3, u3, halo3, halo_index, gp3, ga3, attn3, w_pool_bf16, pool_scale,
             w_out, g2, w_up, w_down, *, row_tile, base_pos):
    nseq, s, d = x3.shape
    pool_w = u3.shape[-1]
    d_ff = w_up.shape[1]
    assert s % row_tile == 0 and row_tile % V7X_SUBLANES == 0 and d_ff % FF_CHUNK == 0
    wide = pl.BlockSpec((None, row_tile, d), lambda q, i: (q, i, 0))
    weights = _nbytes((d, d), BF16) + 2 * _nbytes((d, d_ff), BF16) + _nbytes(w_pool_bf16.shape, BF16)
    return pl.pallas_call(
        functools.partial(_mix_mlp_kernel, base_pos=base_pos),
        grid=(nseq, s // row_tile),
        in_specs=([wide] + _pool_specs(row_tile, pool_w, d, halo_index, w_pool_bf16.shape)
                  + _mlp_weight_specs(d, d_ff, lambda q, i: (0, 0))),
        out_specs=wide,
        out_shape=jax.ShapeDtypeStruct((nseq, s, d), F32),
        scratch_shapes=[pltpu.VMEM((POOL_HALO + row_tile, pool_w), F32)],
        compiler_params=pltpu.CompilerParams(
            dimension_semantics=("arbitrary", "arbitrary"),
            vmem_limit_bytes=_vmem_limit(
                5 * _nbytes((row_tile, d), F32) + _nbytes((row_tile, pool_w), F32),
                2 * weights, 8 * _nbytes((row_tile, FF_CHUNK), F32))),
    )(x3, u3, halo3, gp3, ga3, attn3, w_pool_bf16, pool_scale, w_out, g2, w_up, w_down)


def kernel(x_prompt, x_sample, cache_k, cache_v, state_pool, page_table, norm1_g, w_in,
           q_norm_g, k_norm_g, w_pool, pool_scale, w_out, norm2_g, w_up, w_down):
    depth = w_in.shape[0]
    b, s, d = x_prompt.shape
    db, t_new, _ = x_sample.shape
    dh = d // N_HEADS
    n_pages = page_table.shape[1]
    past = n_pages * PAGE_SIZE
    scale = dh ** -0.5
    pool_buf = state_pool.shape[2]
    assert pool_buf == POOL_HALO - 1
    assert n_pages % (MOBA_BLOCK // PAGE_SIZE) == 0, "cached rows must fill whole MoBA blocks"
    n_cached = n_pages // (MOBA_BLOCK // PAGE_SIZE)

    xp = x_prompt
    xs = x_sample.reshape(db * t_new, d)
    outs = [[] for _ in range(6)]
    for l in range(depth):
        w_in_l = w_in[l].astype(BF16)
        w_pool_l = w_pool[l].astype(BF16)
        w_out_l = w_out[l].astype(BF16)
        w_up_l = w_up[l].astype(BF16)
        w_down_l = w_down[l].astype(BF16)
        g1, g2 = norm1_g[l][None], norm2_g[l][None]
        qg, kg = q_norm_g[l][None], k_norm_g[l][None]
        ps = pool_scale[l][None]

        k, v, u, gp, ga, qs, bias, kb, vt = _project_in(
            xp, g1, w_in_l, qg, kg, row_tile=MOBA_BLOCK, prompt_extras=True, scale=scale)
        attn, cache_ksum = _moba_prompt(qs, bias, kb, vt, page_table.reshape(-1), cache_k[l])
        tile = MOBA_BLOCK
        halo_blocks = tile // POOL_HALO
        xp = _mix_mlp(
            xp, u, u, lambda qi, i: (qi, jnp.maximum(i * halo_blocks - 1, 0), 0), gp, ga, attn,
            w_pool_l, ps, w_out_l, g2, w_up_l, w_down_l, row_tile=tile, base_pos=0)
        outs[0].append(k.reshape(b, s, N_HEADS, dh))
        outs[1].append(v.reshape(b, s, N_HEADS, dh))
        outs[2].append(u[:, s - pool_buf:])

        q, k, v, u, gp, ga = _project_in(
            xs[None], g1, w_in_l, qg, kg, row_tile=db * t_new, prompt_extras=False, scale=scale)
        tok3 = lambda a: a.reshape(db, t_new, a.shape[-1])
        picked = _sample_select(q.reshape(db, t_new, N_HEADS, dh),
                                cache_ksum[:db * n_cached].reshape(db, n_cached, N_HEADS, dh))
        sel = picked[:, :, :, :MOBA_TOP_K].reshape(db, t_new * N_HEADS * MOBA_TOP_K)
        attn = _sample_attend(page_table, sel, tok3(q), tok3(k), tok3(v), cache_k[l], cache_v[l],
                              scale=scale)
        state = state_pool[l].astype(F32)
        halo = jnp.pad(state, ((0, 0), (POOL_HALO - pool_buf, 0), (0, 0)))
        mixed = _pool_mix(
            tok3(u), halo, lambda qi, i: (qi, 0, 0), tok3(gp), tok3(ga), attn, w_pool_l, ps,
            row_tile=t_new, base_pos=past)
        xs = _out_mlp(xs, mixed.reshape(db * t_new, d), w_out_l, g2, w_up_l, w_down_l,
                      row_tile=db * t_new)
        outs[3].append(k.reshape(db, t_new, N_HEADS, dh))
        outs[4].append(v.reshape(db, t_new, N_HEADS, dh))
        outs[5].append(jnp.concatenate([state, tok3(u)], axis=1)[:, -pool_buf:])

    stacked = [jnp.stack(o, axis=0) for o in outs]
    return (xp, xs.reshape(db, t_new, d), *stacked)
```

```python
import functools

import jax
import jax.numpy as jnp
from jax import lax
from jax.experimental import pallas as pl
from jax.experimental.pallas import tpu as pltpu

PAGE_SIZE = 128
N_HEADS = 8
MOBA_BLOCK = 256
MOBA_TOP_K = 3
POOL_WINDOWS = (2, 4, 8, 16)
N_POOL_GROUPS = len(POOL_WINDOWS)
EPS = 1e-6
NEG_INF = -1e30
PICKED = -3e38
LOG2_E = 1.4426950408889634

V7X_LANES = 128
V7X_SUBLANES = 8
V7X_VMEM_BYTES = 64 * 1024 * 1024

POOL_HALO = 16
FF_CHUNK = 1024
HEAD_BIAS_LANES = V7X_LANES // N_HEADS
MOBA_QUAD = 2
MOBA_HEADS_PER_STEP = 2
PARTIAL_SUMS = 8
RIDER_MLP_SHARE = 4
GATHER_SLOTS = 4
GATHER_AHEAD = 3

F32 = jnp.float32
BF16 = jnp.bfloat16


def _vmem_limit(pipelined_bytes, resident_bytes, live_bytes):
    est = 2 * pipelined_bytes + resident_bytes + live_bytes
    return int(min(max(est, 16 * 1024 * 1024), V7X_VMEM_BYTES - 8 * 1024 * 1024))


def _nbytes(shape, dtype):
    n = 1
    for s in shape:
        n *= s
    return n * jnp.dtype(dtype).itemsize


def _rms_scale(x):
    return lax.rsqrt(jnp.mean(x * x, axis=-1, keepdims=True) + EPS)


def _dot_nt(a, b, precision=None):
    return lax.dot_general(a, b, (((1,), (1,)), ((), ())), precision=precision,
                           preferred_element_type=F32)


def _dot_nt_3pass(a, b):
    a_hi = a.astype(BF16)
    b_hi = b.astype(BF16)
    a_lo = (a - a_hi.astype(F32)).astype(BF16)
    b_lo = (b - b_hi.astype(F32)).astype(BF16)
    return _dot_nt(a_hi, b_hi) + (_dot_nt(a_hi, b_lo) + _dot_nt(a_lo, b_hi))


class _PageRider:
    def __init__(self, pt_ref, ck_ref, pbuf, psem, ksum_ref, *, step, n_steps, n_steps_static,
                 pages_per_step, first_page, total_pages):
        self.pt_ref, self.ck_ref, self.pbuf, self.psem, self.ksum_ref = pt_ref, ck_ref, pbuf, psem, ksum_ref
        self.step, self.n_steps, self.pps = step, n_steps, pages_per_step
        self.first_page, self.total_pages = first_page, total_pages
        self.exact_fit = n_steps_static * pages_per_step == total_pages

    def _copy(self, of_step, k, slot):
        page = self.pt_ref[self.first_page + of_step * self.pps + k]
        return pltpu.make_async_copy(self.ck_ref.at[page], self.pbuf.at[slot, k], self.psem.at[slot, k])

    def _for_pages(self, of_step, fn):
        for k in range(self.pps):
            if self.exact_fit:
                fn(k)
            else:
                pl.when(of_step * self.pps + k < self.total_pages)(functools.partial(fn, k))

    def start(self):
        step, slot = self.step, self.step % 2

        @pl.when(step == 0)
        def _():
            self._for_pages(0, lambda k: self._copy(0, k, 0).start())

        @pl.when(step + 1 < self.n_steps)
        def _():
            self._for_pages(step + 1, lambda k: self._copy(step + 1, k, 1 - slot).start())

    def drain(self):
        step, slot = self.step, self.step % 2
        self._for_pages(step, lambda k: self._copy(step, k, slot).wait())
        if not self.exact_fit:
            for k in range(self.pps):
                @pl.when(step * self.pps + k >= self.total_pages)
                def _():
                    self.pbuf[slot, k] = jnp.zeros(self.pbuf.shape[2:], F32)
        n_out, page_heads, dh = self.ksum_ref.shape
        rows = self.pps * PAGE_SIZE // n_out
        pages = self.pbuf[slot].reshape(n_out, PARTIAL_SUMS, rows // PARTIAL_SUMS, page_heads, dh)
        self.ksum_ref[...] = jnp.sum(jnp.sum(pages, axis=2), axis=1)


def _rider_plan(first_page, n_pages, n_steps, page_shape):
    pages_per_block = MOBA_BLOCK // PAGE_SIZE
    assert n_pages > 0 and n_pages % pages_per_block == 0
    blocks_per_step = -(-(n_pages // pages_per_block) // n_steps)
    pages_per_step = blocks_per_step * pages_per_block
    args = dict(n_steps_static=n_steps, pages_per_step=pages_per_step, first_page=first_page,
                total_pages=n_pages)
    scratch = [pltpu.VMEM((2, pages_per_step) + page_shape, F32),
               pltpu.SemaphoreType.DMA((2, pages_per_step))]
    block = (blocks_per_step,) + page_shape[1:]
    out = jax.ShapeDtypeStruct((n_steps * blocks_per_step,) + page_shape[1:], F32)
    return args, scratch, block, out


def _project_in_kernel(x_ref, g1_ref, w_ref, qg_ref, kg_ref, *refs, d_model, prompt_extras, scale):
    if prompt_extras:
        k_ref, v_ref, u_ref, gp_ref, ga_ref, qs_ref, bias_ref, kb_ref, vt_ref, kmean_ref = refs
    else:
        q_ref, k_ref, v_ref, u_ref, gp_ref, ga_ref = refs
    dh = d_model // N_HEADS
    pool_w = u_ref.shape[-1]
    rows = x_ref.shape[0]
    o1, o2, o3 = d_model, 2 * d_model, 3 * d_model
    o4 = o3 + pool_w
    o5 = o4 + d_model
    o6 = o5 + d_model

    x = x_ref[...]
    xn = (x * _rms_scale(x) * g1_ref[...]).astype(BF16)

    def proj(c0, c1):
        return jnp.dot(xn, w_ref[:, c0:c1], preferred_element_type=F32)

    if prompt_extras:
        i = pl.program_id(1)

        @pl.when(jnp.logical_and(pl.program_id(0) == 0, i == 0))
        def _():
            kmean_ref[...] = jnp.zeros(kmean_ref.shape, F32)

    zq = proj(0, o1)
    gates = []
    for h in range(N_HEADS):
        sl = slice(h * dh, (h + 1) * dh)
        zh = zq[:, sl]
        qh = zh * _rms_scale(zh) * qg_ref[...]
        if prompt_extras:
            qs_ref[:, sl] = (qh * (scale * LOG2_E)).astype(BF16)
            gates.append(_dot_nt_3pass(kmean_ref[:, sl], qh))
        else:
            q_ref[:, sl] = qh

    if prompt_extras:
        g = jnp.stack(gates, axis=0)
        blk = lax.broadcasted_iota(jnp.int32, g.shape, 1)
        past = blk < i
        g = jnp.where(past, g, NEG_INF)
        sel = jnp.zeros(g.shape, jnp.bool_)
        for _ in range(MOBA_TOP_K):
            m = jnp.max(g, axis=1, keepdims=True)
            first = jnp.min(jnp.where(g == m, blk, HEAD_BIAS_LANES), axis=1, keepdims=True)
            pick = blk == first
            sel = jnp.logical_or(sel, pick)
            g = jnp.where(pick, PICKED, g)
        bias_t = jnp.where(jnp.logical_and(sel, past), 0.0, NEG_INF)
        bias_ref[...] = bias_t.reshape(V7X_LANES, rows).T.astype(BF16)

    zk = proj(o1, o2)
    for h in range(N_HEADS):
        sl = slice(h * dh, (h + 1) * dh)
        zh = zk[:, sl]
        kh = zh * _rms_scale(zh) * kg_ref[...]
        k_ref[:, sl] = kh
        if prompt_extras:
            kb_ref[:, sl] = kh.astype(BF16)
            block_row = lax.broadcasted_iota(jnp.int32, (HEAD_BIAS_LANES, dh), 0) == i
            kmean_ref[:, sl] = jnp.where(
                block_row, jnp.sum(kh, axis=0, keepdims=True) * (1.0 / MOBA_BLOCK), kmean_ref[:, sl])
    zv = proj(o2, o3)
    v_ref[...] = zv
    if prompt_extras:
        vt_ref[...] = zv.T.astype(BF16)
    u_ref[...] = proj(o3, o4)
    gp_ref[...] = proj(o4, o5)
    ga_ref[...] = proj(o5, o6)


def _project_in(x3, g1, w_bf16, qg, kg, *, row_tile, prompt_extras, scale):
    nseq, s, d = x3.shape
    in_w = w_bf16.shape[1]
    pool_w = in_w - 5 * d
    dh = d // N_HEADS
    assert s % row_tile == 0
    tiles = s // row_tile
    row = lambda width: pl.BlockSpec((None, row_tile, width), lambda q, i: (q, i, 0))
    const = lambda shape: pl.BlockSpec(shape, lambda q, i: (0, 0))
    wide = jax.ShapeDtypeStruct((nseq, s, d), F32)
    scratch = []
    if prompt_extras:
        assert row_tile == MOBA_BLOCK and tiles <= HEAD_BIAS_LANES and dh == V7X_LANES
        out_shape = [wide, wide, jax.ShapeDtypeStruct((nseq, s, pool_w), F32), wide, wide,
                     jax.ShapeDtypeStruct((nseq, s, d), BF16),
                     jax.ShapeDtypeStruct((nseq, s, V7X_LANES), BF16),
                     jax.ShapeDtypeStruct((nseq, s, d), BF16),
                     jax.ShapeDtypeStruct((nseq, d, s), BF16)]
        out_specs = [row(d), row(d), row(pool_w), row(d), row(d), row(d), row(V7X_LANES), row(d),
                     pl.BlockSpec((None, d, row_tile), lambda q, i: (q, 0, i))]
        tile_bytes = (_nbytes((row_tile, d), F32) * 5 + _nbytes((row_tile, pool_w), F32)
                      + 4 * _nbytes((row_tile, d), BF16))
        scratch = [pltpu.VMEM((HEAD_BIAS_LANES, d), F32)]
    else:
        out_shape = [wide] * 3 + [jax.ShapeDtypeStruct((nseq, s, pool_w), F32)] + [wide] * 2
        out_specs = [row(d)] * 3 + [row(pool_w)] + [row(d)] * 2
        tile_bytes = _nbytes((row_tile, d), F32) * 6 + _nbytes((row_tile, pool_w), F32)
    return pl.pallas_call(
        functools.partial(_project_in_kernel, d_model=d, prompt_extras=prompt_extras, scale=scale),
        grid=(nseq, tiles),
        in_specs=[row(d), const((1, d)), const((d, in_w)), const((1, dh)), const((1, dh))],
        out_specs=out_specs,
        out_shape=out_shape,
        scratch_shapes=scratch,
        compiler_params=pltpu.CompilerParams(
            dimension_semantics=("arbitrary", "arbitrary"),
            vmem_limit_bytes=_vmem_limit(tile_bytes, 2 * _nbytes((d, in_w), BF16),
                                         6 * _nbytes((row_tile, d), F32))),
    )(x3, g1, w_bf16, qg, kg)


def _moba_prompt_kernel(pt_ref, qs_ref, bias_ref, kb_ref, vt_ref, vt_own_ref, ck_ref, o_ref, ksum_ref,
                        kaug_ref, s_ref, pbuf, psem, *, rider_args):
    hg = pl.program_id(1)
    i = pl.program_id(2)
    seq = kb_ref.shape[0]
    n_heads = kaug_ref.shape[0]
    dh = kb_ref.shape[1] // n_heads
    n_blocks = seq // MOBA_BLOCK
    pair = 2 * MOBA_BLOCK

    n_steps = pl.num_programs(0) * pl.num_programs(1) * pl.num_programs(2)
    step = (pl.program_id(0) * pl.num_programs(1) + hg) * pl.num_programs(2) + i
    rider = _PageRider(pt_ref, ck_ref, pbuf, psem, ksum_ref, step=step, n_steps=n_steps, **rider_args)
    rider.start()

    @pl.when(i == 0)
    def _():
        row_blk = lax.broadcasted_iota(jnp.int32, (seq, V7X_LANES), 0) // MOBA_BLOCK
        lane = lax.broadcasted_iota(jnp.int32, (seq, V7X_LANES), 1)
        for hh in range(n_heads):
            head = hg * n_heads + hh
            kaug_ref[hh, :, :dh] = kb_ref[:, hh * dh:(hh + 1) * dh]
            kaug_ref[hh, :, dh:] = jnp.where(lane == head * HEAD_BIAS_LANES + row_blk,
                                             1.0, 0.0).astype(BF16)

    own = pl.multiple_of(i * MOBA_BLOCK, MOBA_BLOCK)
    bias = bias_ref[...]
    heads = range(n_heads)
    cols = [slice(hh * dh, (hh + 1) * dh) for hh in heads]
    qs = [qs_ref[:, cols[hh]] for hh in heads]
    q_aug = [jnp.concatenate([qs[hh], bias], axis=1) for hh in heads]

    def attend(n_past):
        rider.drain()
        chunks = [slice(c * pair, (c + 1) * pair) for c in range(n_past // 2)]
        own_rows = slice(n_past * MOBA_BLOCK, (n_past + 1) * MOBA_BLOCK)
        m = []
        for hh in heads:
            s = _dot_nt(kb_ref[pl.ds(own, MOBA_BLOCK), cols[hh]], qs[hh])
            key_i = lax.broadcasted_iota(jnp.int32, s.shape, 0)
            qry_i = lax.broadcasted_iota(jnp.int32, s.shape, 1)
            s = jnp.where(key_i <= qry_i, s, NEG_INF)
            s_ref[hh, own_rows, :] = s
            m.append(jnp.max(s, axis=0, keepdims=True))
        for rows in chunks:
            for hh in heads:
                s = _dot_nt(kaug_ref[hh, rows, :], q_aug[hh])
                s_ref[hh, rows, :] = s
                m[hh] = jnp.maximum(m[hh], jnp.max(s, axis=0, keepdims=True))
        l, acc = [], []
        for hh in heads:
            p = jnp.exp2(s_ref[hh, own_rows, :] - m[hh])
            l.append(jnp.sum(p, axis=0, keepdims=True))
            acc.append(jnp.dot(vt_own_ref[cols[hh], :], p.astype(BF16),
                               preferred_element_type=F32))
        for rows in chunks:
            for hh in heads:
                p = jnp.exp2(s_ref[hh, rows, :] - m[hh])
                l[hh] = l[hh] + jnp.sum(p, axis=0, keepdims=True)
                acc[hh] = acc[hh] + jnp.dot(vt_ref[cols[hh], rows], p.astype(BF16),
                                            preferred_element_type=F32)
        for hh in heads:
            o_ref[:, cols[hh]] = (acc[hh] / l[hh]).T

    n_quads = (i + MOBA_QUAD - 1) // MOBA_QUAD
    for quads in range((n_blocks - 1 + MOBA_QUAD - 1) // MOBA_QUAD + 1):
        pl.when(n_quads == quads)(functools.partial(attend, min(quads * MOBA_QUAD, n_blocks)))


def _moba_prompt(qs, bias, kb, vt, page_list, cache_k, first_page, n_pages):
    b, s, d = qs.shape
    dh = d // N_HEADS
    assert s % (2 * MOBA_BLOCK) == 0 and dh == V7X_LANES
    n_blocks = s // MOBA_BLOCK
    assert n_blocks <= HEAD_BIAS_LANES
    tq = MOBA_BLOCK
    hps = MOBA_HEADS_PER_STEP
    hw = hps * dh
    head_groups = N_HEADS // hps
    rider_args, rider_scratch, ksum_block, ksum_out = _rider_plan(
        first_page, n_pages, b * head_groups * n_blocks, cache_k.shape[1:])
    tile = pl.BlockSpec((None, tq, hw), lambda bi, h, i, pt: (bi, i, h))
    grid_spec = pltpu.PrefetchScalarGridSpec(
        num_scalar_prefetch=1,
        grid=(b, head_groups, n_blocks),
        in_specs=[tile,
                  pl.BlockSpec((None, tq, V7X_LANES), lambda bi, h, i, pt: (bi, i, 0)),
                  pl.BlockSpec((None, s, hw), lambda bi, h, i, pt: (bi, 0, h)),
                  pl.BlockSpec((None, hw, s), lambda bi, h, i, pt: (bi, h, 0)),
                  pl.BlockSpec((None, hw, tq), lambda bi, h, i, pt: (bi, h, i)),
                  pl.BlockSpec(memory_space=pl.ANY)],
        out_specs=[tile,
                   pl.BlockSpec(ksum_block,
                                lambda bi, h, i, pt: ((bi * head_groups + h) * n_blocks + i, 0, 0))],
        scratch_shapes=[pltpu.VMEM((hps, s, dh + V7X_LANES), BF16),
                        pltpu.VMEM((hps, s + MOBA_BLOCK, tq), F32)] + rider_scratch)
    return pl.pallas_call(
        functools.partial(_moba_prompt_kernel, rider_args=rider_args),
        grid_spec=grid_spec,
        out_shape=[jax.ShapeDtypeStruct((b, s, d), F32), ksum_out],
        compiler_params=pltpu.CompilerParams(
            dimension_semantics=("arbitrary", "arbitrary", "arbitrary"),
            vmem_limit_bytes=_vmem_limit(
                2 * _nbytes((s, hw), BF16) + 3 * _nbytes((tq, hw), F32) + _nbytes(ksum_block, F32),
                _nbytes((hps, s, dh + V7X_LANES), BF16) + _nbytes((hps, s + MOBA_BLOCK, tq), F32)
                + _nbytes(rider_scratch[0].shape, F32),
                16 * hps * _nbytes((2 * MOBA_BLOCK, tq), F32))),
    )(page_list, qs, bias, kb, vt, vt, cache_k)


def _sample_select_kernel(q_ref, ksum_ref, idx_ref):
    n_blocks, n_heads, dh = ksum_ref.shape
    t_new = q_ref.shape[0]
    kmean = ksum_ref[...] * (1.0 / MOBA_BLOCK)
    blk = lax.broadcasted_iota(jnp.int32, (n_blocks, n_heads, 1), 0).astype(F32)
    lane = lax.broadcasted_iota(jnp.int32, (n_heads, V7X_LANES), 1)
    for t in range(t_new):
        g = jnp.sum(kmean * q_ref[t][None], axis=-1, keepdims=True)
        tile = jnp.zeros((n_heads, V7X_LANES), F32)
        for r in range(MOBA_TOP_K):
            m = jnp.max(g, axis=0, keepdims=True)
            first = jnp.min(jnp.where(g == m, blk, float(n_blocks)), axis=0, keepdims=True)
            tile = jnp.where(lane == r, first[0], tile)
            g = jnp.where(blk == first, PICKED, g)
        idx_ref[t] = tile.astype(jnp.int32)


def _sample_select(q4, ksum4):
    db, t_new, n_heads, dh = q4.shape
    n_blocks = ksum4.shape[1]
    assert n_blocks >= MOBA_TOP_K and n_heads == V7X_SUBLANES and dh == V7X_LANES
    return pl.pallas_call(
        _sample_select_kernel,
        grid=(db,),
        in_specs=[pl.BlockSpec((None, t_new, n_heads, dh), lambda b: (b, 0, 0, 0)),
                  pl.BlockSpec((None, n_blocks, n_heads, dh), lambda b: (b, 0, 0, 0))],
        out_specs=pl.BlockSpec((None, t_new, n_heads, V7X_LANES), lambda b: (b, 0, 0, 0)),
        out_shape=jax.ShapeDtypeStruct((db, t_new, n_heads, V7X_LANES), jnp.int32),
        compiler_params=pltpu.CompilerParams(
            dimension_semantics=("arbitrary",),
            vmem_limit_bytes=_vmem_limit(
                _nbytes((n_blocks, n_heads, dh), F32) + 2 * _nbytes((t_new, n_heads, V7X_LANES), F32),
                0, 8 * _nbytes((n_blocks, n_heads, dh), F32))),
    )(q4, ksum4)


def _sample_attend_kernel(pt_ref, sel_ref, q_ref, kn_ref, vn_ref, ck_ref, cv_ref, o_ref,
                          kg, vg, sem, *, pages_per_block, scale):
    b = pl.program_id(0)
    n_seqs = pl.num_programs(0)
    t_new, d = q_ref.shape
    dh = d // N_HEADS
    rows_per_token = MOBA_TOP_K * MOBA_BLOCK

    def unit_copies(seq, h):
        slot = h % GATHER_SLOTS
        copies = []
        for t in range(t_new):
            for r in range(MOBA_TOP_K):
                blk = sel_ref[seq, (t * N_HEADS + h) * MOBA_TOP_K + r]
                for pp in range(pages_per_block):
                    phys = pt_ref[seq, blk * pages_per_block + pp]
                    row0 = ((t * MOBA_TOP_K + r) * pages_per_block + pp) * PAGE_SIZE
                    dst = pl.ds(row0, PAGE_SIZE)
                    copies.append(pltpu.make_async_copy(
                        ck_ref.at[phys, :, h, :], kg.at[slot, dst, :], sem.at[0, slot]))
                    copies.append(pltpu.make_async_copy(
                        cv_ref.at[phys, :, h, :], vg.at[slot, dst, :], sem.at[1, slot]))
        return copies

    def start_unit(seq, h):
        for c in unit_copies(seq, h):
            c.start()

    @pl.when(b == 0)
    def _():
        for h in range(GATHER_AHEAD):
            start_unit(0, h)

    q = q_ref[...]
    kn = kn_ref[...]
    vn = vn_ref[...]
    own_row = lax.broadcasted_iota(jnp.int32, (t_new, 1), 0)

    for h in range(N_HEADS):
        ahead = h + GATHER_AHEAD
        if ahead < N_HEADS:
            start_unit(b, ahead)
        else:
            pl.when(b + 1 < n_seqs)(functools.partial(start_unit, b + 1, ahead - N_HEADS))
        for c in unit_copies(b, h):
            c.wait()
        slot = h % GATHER_SLOTS
        sl = slice(h * dh, (h + 1) * dh)
        kn_h = kn[:, sl]
        vn_h = vn[:, sl]
        for t in range(t_new):
            rows = pl.ds(t * rows_per_token, rows_per_token)
            qrow = q[t:t + 1, sl] * scale
            s = jnp.sum(kg[slot, rows, :] * qrow, axis=-1, keepdims=True)
            s_own = jnp.sum(kn_h * qrow, axis=-1, keepdims=True)
            s_own = jnp.where(own_row <= t, s_own, NEG_INF)
            m = jnp.maximum(jnp.max(s, axis=0, keepdims=True), jnp.max(s_own, axis=0, keepdims=True))
            p = jnp.exp(s - m)
            p_own = jnp.exp(s_own - m)
            denom = jnp.sum(p, axis=0, keepdims=True) + jnp.sum(p_own, axis=0, keepdims=True)
            o = (jnp.sum(p * vg[slot, rows, :], axis=0, keepdims=True)
                 + jnp.sum(p_own * vn_h, axis=0, keepdims=True))
            o_ref[t:t + 1, sl] = o / denom


def _sample_attend(page_table, sel, q3, kn3, vn3, cache_k, cache_v, *, scale):
    db, t_new, d = q3.shape
    dh = d // N_HEADS
    pages_per_block = MOBA_BLOCK // PAGE_SIZE
    assert N_HEADS % GATHER_SLOTS == 0 and GATHER_AHEAD < GATHER_SLOTS
    gathered = (GATHER_SLOTS, t_new * MOBA_TOP_K * MOBA_BLOCK, dh)
    tok = pl.BlockSpec((None, t_new, d), lambda b, pt, sel: (b, 0, 0))
    grid_spec = pltpu.PrefetchScalarGridSpec(
        num_scalar_prefetch=2,
        grid=(db,),
        in_specs=[tok, tok, tok, pl.BlockSpec(memory_space=pl.ANY), pl.BlockSpec(memory_space=pl.ANY)],
        out_specs=tok,
        scratch_shapes=[pltpu.VMEM(gathered, F32), pltpu.VMEM(gathered, F32),
                        pltpu.SemaphoreType.DMA((2, GATHER_SLOTS))])
    return pl.pallas_call(
        functools.partial(_sample_attend_kernel, pages_per_block=pages_per_block, scale=scale),
        grid_spec=grid_spec,
        out_shape=jax.ShapeDtypeStruct((db, t_new, d), F32),
        compiler_params=pltpu.CompilerParams(
            dimension_semantics=("arbitrary",),
            vmem_limit_bytes=_vmem_limit(4 * _nbytes((V7X_SUBLANES, d), F32),
                                         2 * _nbytes(gathered, F32),
                                         8 * _nbytes((MOBA_TOP_K * MOBA_BLOCK, dh), F32))),
    )(page_table, sel, q3, kn3, vn3, cache_k, cache_v)


def _pool_mix_tile(u_ref, halo_ref, gp_ref, ga_ref, attn_ref, wp_ref, ps_ref, uext, pos0):
    t_rows, pool_w = u_ref.shape
    t_pad = uext.shape[0] - POOL_HALO
    gdim = pool_w // N_POOL_GROUPS
    gout = wp_ref.shape[-1]

    halo_pos = pos0 - POOL_HALO + lax.broadcasted_iota(jnp.int32, (POOL_HALO, 1), 0)
    uext[0:POOL_HALO, :] = jnp.where(halo_pos >= 0, halo_ref[...], 0.0)
    uext[POOL_HALO:POOL_HALO + t_rows, :] = u_ref[...]
    if t_pad > t_rows:
        uext[POOL_HALO + t_rows:, :] = jnp.zeros((t_pad - t_rows, pool_w), F32)

    pos = pos0 + lax.broadcasted_iota(jnp.int32, (t_pad, 1), 0)
    slabs = []
    for g, window in enumerate(POOL_WINDOWS):
        cols = slice(g * gdim, (g + 1) * gdim)
        tok = uext[POOL_HALO:POOL_HALO + t_pad, cols]
        total = tok
        for r in range(1, window):
            total = total + uext[POOL_HALO - r:POOL_HALO - r + t_pad, cols]
        cnt = jnp.minimum(window, pos + 1).astype(F32)
        diff = total / cnt - tok
        y = jnp.dot(diff.astype(BF16), wp_ref[g], preferred_element_type=F32)[:t_rows]
        oc = slice(g * gout, (g + 1) * gout)
        pool = y * ps_ref[:, oc]
        slabs.append(jax.nn.sigmoid(gp_ref[:, oc]) * pool
                     + jax.nn.sigmoid(ga_ref[:, oc]) * attn_ref[:, oc])
    return slabs


def _mlp_tile(x, mixed, wo_ref, g2_ref, wu_ref, wd_ref):
    d_ff = wu_ref.shape[1]
    hres = x + jnp.dot(mixed.astype(BF16), wo_ref[...], preferred_element_type=F32)
    hn = (hres * _rms_scale(hres) * g2_ref[...]).astype(BF16)
    acc = hres
    for c in range(d_ff // FF_CHUNK):
        cols = slice(c * FF_CHUNK, (c + 1) * FF_CHUNK)
        a = jnp.dot(hn, wu_ref[:, cols], preferred_element_type=F32)
        a = jnp.square(jnp.maximum(a, 0.0)).astype(BF16)
        acc = acc + jnp.dot(a, wd_ref[cols, :], preferred_element_type=F32)
    return acc


def _pool_mix_kernel(u_ref, halo_ref, gp_ref, ga_ref, attn_ref, wp_ref, ps_ref, o_ref, uext, *,
                     base_pos):
    pos0 = base_pos + pl.program_id(1) * u_ref.shape[0]
    slabs = _pool_mix_tile(u_ref, halo_ref, gp_ref, ga_ref, attn_ref, wp_ref, ps_ref, uext, pos0)
    gout = wp_ref.shape[-1]
    for g, slab in enumerate(slabs):
        o_ref[:, g * gout:(g + 1) * gout] = slab


def _out_mlp_kernel(x_ref, mixed_ref, wo_ref, g2_ref, wu_ref, wd_ref, y_ref):
    y_ref[...] = _mlp_tile(x_ref[...], mixed_ref[...], wo_ref, g2_ref, wu_ref, wd_ref)


def _mix_mlp_kernel(pt_ref, x_ref, u_ref, halo_ref, gp_ref, ga_ref, attn_ref, wp_ref, ps_ref,
                    wo_ref, g2_ref, wu_ref, wd_ref, ck_ref, y_ref, ksum_ref, uext, pbuf, psem, *,
                    base_pos, rider_args):
    step = pl.program_id(0) * pl.num_programs(1) + pl.program_id(1)
    rider = _PageRider(pt_ref, ck_ref, pbuf, psem, ksum_ref, step=step,
                       n_steps=pl.num_programs(0) * pl.num_programs(1), **rider_args)
    rider.start()
    rider.drain()
    pos0 = base_pos + pl.program_id(1) * u_ref.shape[0]
    slabs = _pool_mix_tile(u_ref, halo_ref, gp_ref, ga_ref, attn_ref, wp_ref, ps_ref, uext, pos0)
    mixed = jnp.concatenate(slabs, axis=1)
    y_ref[...] = _mlp_tile(x_ref[...], mixed, wo_ref, g2_ref, wu_ref, wd_ref)


def _pool_specs(row_tile, pool_w, d, halo_index, w_pool_shape):
    at = lambda f: (lambda q, i, *prefetch: f(q, i))
    wide = pl.BlockSpec((None, row_tile, d), at(lambda q, i: (q, i, 0)))
    return [pl.BlockSpec((None, row_tile, pool_w), at(lambda q, i: (q, i, 0))),
            pl.BlockSpec((None, POOL_HALO, pool_w), at(halo_index)),
            wide, wide, wide,
            pl.BlockSpec(w_pool_shape, at(lambda q, i: (0, 0, 0))),
            pl.BlockSpec((1, d), at(lambda q, i: (0, 0)))]


def _pool_mix(u3, halo3, halo_index, gp3, ga3, attn3, w_pool_bf16, pool_scale, *, row_tile, base_pos):
    nseq, s, pool_w = u3.shape
    d = gp3.shape[-1]
    assert s % row_tile == 0
    t_pad = -(-row_tile // V7X_SUBLANES) * V7X_SUBLANES
    return pl.pallas_call(
        functools.partial(_pool_mix_kernel, base_pos=base_pos),
        grid=(nseq, s // row_tile),
        in_specs=_pool_specs(row_tile, pool_w, d, halo_index, w_pool_bf16.shape),
        out_specs=pl.BlockSpec((None, row_tile, d), lambda q, i: (q, i, 0)),
        out_shape=jax.ShapeDtypeStruct((nseq, s, d), F32),
        scratch_shapes=[pltpu.VMEM((POOL_HALO + t_pad, pool_w), F32)],
        compiler_params=pltpu.CompilerParams(
            dimension_semantics=("arbitrary", "arbitrary"),
            vmem_limit_bytes=_vmem_limit(5 * _nbytes((t_pad, d), F32),
                                         _nbytes(w_pool_bf16.shape, BF16),
                                         6 * _nbytes((t_pad, d), F32))),
    )(u3, halo3, gp3, ga3, attn3, w_pool_bf16, pool_scale)


def _mlp_weight_specs(d, d_ff, index):
    return [pl.BlockSpec((d, d), index), pl.BlockSpec((1, d), index),
            pl.BlockSpec((d, d_ff), index), pl.BlockSpec((d_ff, d), index)]


def _out_mlp(x2d, mixed2d, w_out, g2, w_up, w_down, *, row_tile):
    n, d = x2d.shape
    d_ff = w_up.shape[1]
    assert n % row_tile == 0 and d_ff % FF_CHUNK == 0
    row = pl.BlockSpec((row_tile, d), lambda i: (i, 0))
    weights = _nbytes((d, d), BF16) + 2 * _nbytes((d, d_ff), BF16)
    return pl.pallas_call(
        _out_mlp_kernel,
        grid=(n // row_tile,),
        in_specs=[row, row] + _mlp_weight_specs(d, d_ff, lambda i: (0, 0)),
        out_specs=row,
        out_shape=jax.ShapeDtypeStruct((n, d), F32),
        compiler_params=pltpu.CompilerParams(
            dimension_semantics=("arbitrary",),
            vmem_limit_bytes=_vmem_limit(3 * _nbytes((row_tile, d), F32), 2 * weights,
                                         6 * _nbytes((row_tile, FF_CHUNK), F32))),
    )(x2d, mixed2d, w_out, g2, w_up, w_down)


def _mix_mlp(x3, u3, halo3, halo_index, gp3, ga3, attn3, w_pool_bf16, pool_scale,
             w_out, g2, w_up, w_down, page_list, cache_k, first_page, n_pages, *, row_tile, base_pos):
    nseq, s, d = x3.shape
    pool_w = u3.shape[-1]
    d_ff = w_up.shape[1]
    assert s % row_tile == 0 and row_tile % V7X_SUBLANES == 0 and d_ff % FF_CHUNK == 0
    tiles = s // row_tile
    rider_args, rider_scratch, ksum_block, ksum_out = _rider_plan(
        first_page, n_pages, nseq * tiles, cache_k.shape[1:])
    wide = pl.BlockSpec((None, row_tile, d), lambda q, i, pt: (q, i, 0))
    weights = _nbytes((d, d), BF16) + 2 * _nbytes((d, d_ff), BF16) + _nbytes(w_pool_bf16.shape, BF16)
    grid_spec = pltpu.PrefetchScalarGridSpec(
        num_scalar_prefetch=1,
        grid=(nseq, tiles),
        in_specs=([wide] + _pool_specs(row_tile, pool_w, d, halo_index, w_pool_bf16.shape)
                  + _mlp_weight_specs(d, d_ff, lambda q, i, pt: (0, 0))
                  + [pl.BlockSpec(memory_space=pl.ANY)]),
        out_specs=[wide, pl.BlockSpec(ksum_block, lambda q, i, pt: (q * tiles + i, 0, 0))],
        scratch_shapes=[pltpu.VMEM((POOL_HALO + row_tile, pool_w), F32)] + rider_scratch)
    return pl.pallas_call(
        functools.partial(_mix_mlp_kernel, base_pos=base_pos, rider_args=rider_args),
        grid_spec=grid_spec,
        out_shape=[jax.ShapeDtypeStruct((nseq, s, d), F32), ksum_out],
        compiler_params=pltpu.CompilerParams(
            dimension_semantics=("arbitrary", "arbitrary"),
            vmem_limit_bytes=_vmem_limit(
                5 * _nbytes((row_tile, d), F32) + _nbytes((row_tile, pool_w), F32)
                + _nbytes(ksum_block, F32),
                2 * weights + _nbytes(rider_scratch[0].shape, F32),
                8 * _nbytes((row_tile, FF_CHUNK), F32))),
    )(page_list, x3, u3, halo3, gp3, ga3, attn3, w_pool_bf16, pool_scale, w_out, g2, w_up, w_down,
      cache_k)


def kernel(x_prompt, x_sample, cache_k, cache_v, state_pool, page_table, norm1_g, w_in,
           q_norm_g, k_norm_g, w_pool, pool_scale, w_out, norm2_g, w_up, w_down):
    depth = w_in.shape[0]
    b, s, d = x_prompt.shape
    db, t_new, _ = x_sample.shape
    dh = d // N_HEADS
    n_pages = page_table.shape[1]
    past = n_pages * PAGE_SIZE
    scale = dh ** -0.5
    pool_buf = state_pool.shape[2]
    assert pool_buf == POOL_HALO - 1
    assert n_pages % (MOBA_BLOCK // PAGE_SIZE) == 0, "cached rows must fill whole MoBA blocks"
    n_cached = n_pages // (MOBA_BLOCK // PAGE_SIZE)

    xp = x_prompt
    xs = x_sample.reshape(db * t_new, d)
    outs = [[] for _ in range(6)]
    for l in range(depth):
        w_in_l = w_in[l].astype(BF16)
        w_pool_l = w_pool[l].astype(BF16)
        w_out_l = w_out[l].astype(BF16)
        w_up_l = w_up[l].astype(BF16)
        w_down_l = w_down[l].astype(BF16)
        g1, g2 = norm1_g[l][None], norm2_g[l][None]
        qg, kg = q_norm_g[l][None], k_norm_g[l][None]
        ps = pool_scale[l][None]

        k, v, u, gp, ga, qs, bias, kb, vt = _project_in(
            xp, g1, w_in_l, qg, kg, row_tile=MOBA_BLOCK, prompt_extras=True, scale=scale)
        page_list = page_table.reshape(-1)
        pages_per_block = MOBA_BLOCK // PAGE_SIZE
        n_mlp_pages = db * n_pages // RIDER_MLP_SHARE // pages_per_block * pages_per_block
        n_moba_pages = db * n_pages - n_mlp_pages
        attn, ksum_a = _moba_prompt(qs, bias, kb, vt, page_list, cache_k[l], 0, n_moba_pages)
        tile = MOBA_BLOCK
        halo_blocks = tile // POOL_HALO
        xp, ksum_b = _mix_mlp(
            xp, u, u, lambda qi, i: (qi, jnp.maximum(i * halo_blocks - 1, 0), 0), gp, ga, attn,
            w_pool_l, ps, w_out_l, g2, w_up_l, w_down_l, page_list, cache_k[l], n_moba_pages, n_mlp_pages,
            row_tile=tile, base_pos=0)
        cache_ksum = jnp.concatenate([ksum_a[:n_moba_pages // pages_per_block],
                                      ksum_b[:n_mlp_pages // pages_per_block]])
        outs[0].append(k.reshape(b, s, N_HEADS, dh))
        outs[1].append(v.reshape(b, s, N_HEADS, dh))
        outs[2].append(u[:, s - pool_buf:])

        q, k, v, u, gp, ga = _project_in(
            xs[None], g1, w_in_l, qg, kg, row_tile=db * t_new, prompt_extras=False, scale=scale)
        tok3 = lambda a: a.reshape(db, t_new, a.shape[-1])
        picked = _sample_select(q.reshape(db, t_new, N_HEADS, dh),
                                cache_ksum.reshape(db, n_cached, N_HEADS, dh))
        sel = picked[:, :, :, :MOBA_TOP_K].reshape(db, t_new * N_HEADS * MOBA_TOP_K)
        attn = _sample_attend(page_table, sel, tok3(q), tok3(k), tok3(v), cache_k[l], cache_v[l],
                              scale=scale)
        state = state_pool[l].astype(F32)
        halo = jnp.pad(state, ((0, 0), (POOL_HALO - pool_buf, 0), (0, 0)))
        mixed = _pool_mix(
            tok3(u), halo, lambda qi, i: (qi, 0, 0), tok3(gp), tok3(ga), attn, w_pool_l, ps,
            row_tile=t_new, base_pos=past)
        xs = _out_mlp(xs, mixed.reshape(db * t_new, d), w_out_l, g2, w_up_l, w_down_l,
                      row_tile=db * t_new)
        outs[3].append(k.reshape(db, t_new, N_HEADS, dh))
        outs[4].append(v.reshape(db, t_new, N_HEADS, dh))
        outs[5].append(jnp.concatenate([state, tok3(u)], axis=1)[:, -pool_buf:])

    stacked = [jnp.stack(o, axis=0) for o in outs]
    return (xp, xs.reshape(db, t_new, d), *stacked)
```

```python
import functools

import jax
import jax.numpy as jnp
from jax import lax
from jax.experimental import pallas as pl
from jax.experimental.pallas import tpu as pltpu

PAGE_SIZE = 128
N_HEADS = 8
MOBA_BLOCK = 256
MOBA_TOP_K = 3
POOL_WINDOWS = (2, 4, 8, 16)
N_POOL_GROUPS = len(POOL_WINDOWS)
EPS = 1e-6
NEG_INF = -1e30
PICKED = -3e38
LOG2_E = 1.4426950408889634

V7X_LANES = 128
V7X_SUBLANES = 8
V7X_VMEM_BYTES = 64 * 1024 * 1024

POOL_HALO = 16
FF_CHUNK = 1024
HEAD_BIAS_LANES = V7X_LANES // N_HEADS
MOBA_QUAD = 2
MOBA_HEADS_PER_STEP = 2
PARTIAL_SUMS = 8
RIDER_SHARE_PROJECT_IN = 1
RIDER_SHARE_MIX_MLP = 3
RIDER_SHARE_TOTAL = 8
SELECT_SEQS_PER_STEP = 4
GATHER_SLOTS = 8
GATHER_AHEAD = 6

F32 = jnp.float32
BF16 = jnp.bfloat16


def _vmem_limit(pipelined_bytes, resident_bytes, live_bytes):
    est = 2 * pipelined_bytes + resident_bytes + live_bytes
    return int(min(max(est, 16 * 1024 * 1024), V7X_VMEM_BYTES - 8 * 1024 * 1024))


def _nbytes(shape, dtype):
    n = 1
    for s in shape:
        n *= s
    return n * jnp.dtype(dtype).itemsize


def _rms_scale(x):
    return lax.rsqrt(jnp.mean(x * x, axis=-1, keepdims=True) + EPS)


def _dot_nt(a, b, precision=None):
    return lax.dot_general(a, b, (((1,), (1,)), ((), ())), precision=precision,
                           preferred_element_type=F32)


def _dot_nt_3pass(a, b):
    a_hi = a.astype(BF16)
    b_hi = b.astype(BF16)
    a_lo = (a - a_hi.astype(F32)).astype(BF16)
    b_lo = (b - b_hi.astype(F32)).astype(BF16)
    return _dot_nt(a_hi, b_hi) + (_dot_nt(a_hi, b_lo) + _dot_nt(a_lo, b_hi))


class _PageRider:
    def __init__(self, pt_ref, ck_ref, pbuf, psem, ksum_ref, *, step, n_steps, n_steps_static,
                 pages_per_step, first_page, total_pages):
        self.pt_ref, self.ck_ref, self.pbuf, self.psem, self.ksum_ref = pt_ref, ck_ref, pbuf, psem, ksum_ref
        self.step, self.n_steps, self.pps = step, n_steps, pages_per_step
        self.first_page, self.total_pages = first_page, total_pages
        self.exact_fit = n_steps_static * pages_per_step == total_pages

    def _copy(self, of_step, k, slot):
        page = self.pt_ref[self.first_page + of_step * self.pps + k]
        return pltpu.make_async_copy(self.ck_ref.at[page], self.pbuf.at[slot, k], self.psem.at[slot, k])

    def _for_pages(self, of_step, fn):
        for k in range(self.pps):
            if self.exact_fit:
                fn(k)
            else:
                pl.when(of_step * self.pps + k < self.total_pages)(functools.partial(fn, k))

    def start(self):
        step, slot = self.step, self.step % 2

        @pl.when(step == 0)
        def _():
            self._for_pages(0, lambda k: self._copy(0, k, 0).start())

        @pl.when(step + 1 < self.n_steps)
        def _():
            self._for_pages(step + 1, lambda k: self._copy(step + 1, k, 1 - slot).start())

    def drain(self):
        step, slot = self.step, self.step % 2
        self._for_pages(step, lambda k: self._copy(step, k, slot).wait())
        if not self.exact_fit:
            for k in range(self.pps):
                @pl.when(step * self.pps + k >= self.total_pages)
                def _():
                    self.pbuf[slot, k] = jnp.zeros(self.pbuf.shape[2:], F32)
        n_out, page_heads, dh = self.ksum_ref.shape
        rows = self.pps * PAGE_SIZE // n_out
        pages = self.pbuf[slot].reshape(n_out, PARTIAL_SUMS, rows // PARTIAL_SUMS, page_heads, dh)
        self.ksum_ref[...] = jnp.sum(jnp.sum(pages, axis=2), axis=1)


def _rider_plan(first_page, n_pages, n_steps, page_shape):
    pages_per_block = MOBA_BLOCK // PAGE_SIZE
    assert n_pages > 0 and n_pages % pages_per_block == 0
    blocks_per_step = -(-(n_pages // pages_per_block) // n_steps)
    pages_per_step = blocks_per_step * pages_per_block
    args = dict(n_steps_static=n_steps, pages_per_step=pages_per_step, first_page=first_page,
                total_pages=n_pages)
    scratch = [pltpu.VMEM((2, pages_per_step) + page_shape, F32),
               pltpu.SemaphoreType.DMA((2, pages_per_step))]
    block = (blocks_per_step,) + page_shape[1:]
    out = jax.ShapeDtypeStruct((n_steps * blocks_per_step,) + page_shape[1:], F32)
    return args, scratch, block, out


def _resident(shape, index):
    return pl.BlockSpec(shape, index, pipeline_mode=pl.Buffered(1))


def _project_in_kernel(*refs, d_model, prompt_extras, scale, rider_args):
    if prompt_extras:
        (pt_ref, x_ref, g1_ref, w_ref, qg_ref, kg_ref, ck_ref,
         k_ref, v_ref, u_ref, gp_ref, ga_ref, qs_ref, bias_ref, kb_ref, vt_ref, ksum_ref,
         kmean_ref, pbuf, psem) = refs
        step = pl.program_id(0) * pl.num_programs(1) + pl.program_id(1)
        rider = _PageRider(pt_ref, ck_ref, pbuf, psem, ksum_ref, step=step,
                           n_steps=pl.num_programs(0) * pl.num_programs(1), **rider_args)
        rider.start()
        rider.drain()
    else:
        x_ref, g1_ref, w_ref, qg_ref, kg_ref, q_ref, k_ref, v_ref, u_ref, gp_ref, ga_ref = refs
    dh = d_model // N_HEADS
    pool_w = u_ref.shape[-1]
    rows = x_ref.shape[0]
    o1, o2, o3 = d_model, 2 * d_model, 3 * d_model
    o4 = o3 + pool_w
    o5 = o4 + d_model
    o6 = o5 + d_model

    x = x_ref[...]
    xn = (x * _rms_scale(x) * g1_ref[...]).astype(BF16)

    def proj(c0, c1):
        return jnp.dot(xn, w_ref[:, c0:c1], preferred_element_type=F32)

    if prompt_extras:
        i = pl.program_id(1)

        @pl.when(jnp.logical_and(pl.program_id(0) == 0, i == 0))
        def _():
            kmean_ref[...] = jnp.zeros(kmean_ref.shape, F32)

    zq = proj(0, o1)
    gates = []
    for h in range(N_HEADS):
        sl = slice(h * dh, (h + 1) * dh)
        zh = zq[:, sl]
        qh = zh * _rms_scale(zh) * qg_ref[...]
        if prompt_extras:
            qs_ref[:, sl] = (qh * (scale * LOG2_E)).astype(BF16)
            gates.append(_dot_nt_3pass(kmean_ref[:, sl], qh))
        else:
            q_ref[:, sl] = qh

    if prompt_extras:
        g = jnp.stack(gates, axis=0)
        blk = lax.broadcasted_iota(jnp.int32, g.shape, 1)
        past = blk < i
        g = jnp.where(past, g, NEG_INF)
        sel = jnp.zeros(g.shape, jnp.bool_)
        for _ in range(MOBA_TOP_K):
            m = jnp.max(g, axis=1, keepdims=True)
            first = jnp.min(jnp.where(g == m, blk, HEAD_BIAS_LANES), axis=1, keepdims=True)
            pick = blk == first
            sel = jnp.logical_or(sel, pick)
            g = jnp.where(pick, PICKED, g)
        bias_t = jnp.where(jnp.logical_and(sel, past), 0.0, NEG_INF)
        bias_ref[...] = bias_t.reshape(V7X_LANES, rows).T.astype(BF16)

    zk = proj(o1, o2)
    for h in range(N_HEADS):
        sl = slice(h * dh, (h + 1) * dh)
        zh = zk[:, sl]
        kh = zh * _rms_scale(zh) * kg_ref[...]
        k_ref[:, sl] = kh
        if prompt_extras:
            kb_ref[:, sl] = kh.astype(BF16)
            block_row = lax.broadcasted_iota(jnp.int32, (HEAD_BIAS_LANES, dh), 0) == i
            kmean_ref[:, sl] = jnp.where(
                block_row, jnp.sum(kh, axis=0, keepdims=True) * (1.0 / MOBA_BLOCK), kmean_ref[:, sl])
    zv = proj(o2, o3)
    v_ref[...] = zv
    if prompt_extras:
        vt_ref[...] = zv.T.astype(BF16)
    u_ref[...] = proj(o3, o4)
    gp_ref[...] = proj(o4, o5)
    ga_ref[...] = proj(o5, o6)


def _project_in(x3, g1, w_bf16, qg, kg, *, row_tile, scale, rider=None):
    nseq, s, d = x3.shape
    in_w = w_bf16.shape[1]
    pool_w = in_w - 5 * d
    dh = d // N_HEADS
    assert s % row_tile == 0
    tiles = s // row_tile
    prompt_extras = rider is not None
    row = lambda width: pl.BlockSpec((None, row_tile, width), lambda q, i, *_: (q, i, 0))
    const = lambda shape: pl.BlockSpec(shape, lambda q, i, *_: (0, 0))
    in_specs = [row(d), const((1, d)), _resident((d, in_w), lambda q, i, *_: (0, 0)),
                const((1, dh)), const((1, dh))]
    wide = jax.ShapeDtypeStruct((nseq, s, d), F32)
    operands = (x3, g1, w_bf16, qg, kg)
    if prompt_extras:
        assert row_tile == MOBA_BLOCK and tiles <= HEAD_BIAS_LANES and dh == V7X_LANES
        page_list, cache_k, first_page, n_pages = rider
        rider_args, rider_scratch, ksum_block, ksum_out = _rider_plan(
            first_page, n_pages, nseq * tiles, cache_k.shape[1:])
        out_shape = [wide, wide, jax.ShapeDtypeStruct((nseq, s, pool_w), F32), wide, wide,
                     jax.ShapeDtypeStruct((nseq, s, d), BF16),
                     jax.ShapeDtypeStruct((nseq, s, V7X_LANES), BF16),
                     jax.ShapeDtypeStruct((nseq, s, d), BF16),
                     jax.ShapeDtypeStruct((nseq, d, s), BF16),
                     ksum_out]
        out_specs = [row(d), row(d), row(pool_w), row(d), row(d), row(d), row(V7X_LANES), row(d),
                     pl.BlockSpec((None, d, row_tile), lambda q, i, *_: (q, 0, i)),
                     pl.BlockSpec(ksum_block, lambda q, i, *_: (q * tiles + i, 0, 0))]
        tile_bytes = (_nbytes((row_tile, d), F32) * 5 + _nbytes((row_tile, pool_w), F32)
                      + 4 * _nbytes((row_tile, d), BF16) + _nbytes(ksum_block, F32))
        resident = _nbytes((d, in_w), BF16) + _nbytes(rider_scratch[0].shape, F32)
        grid_spec = pltpu.PrefetchScalarGridSpec(
            num_scalar_prefetch=1, grid=(nseq, tiles),
            in_specs=in_specs + [pl.BlockSpec(memory_space=pl.ANY)], out_specs=out_specs,
            scratch_shapes=[pltpu.VMEM((HEAD_BIAS_LANES, d), F32)] + rider_scratch)
        operands = (page_list,) + operands + (cache_k,)
    else:
        rider_args = None
        out_shape = [wide] * 3 + [jax.ShapeDtypeStruct((nseq, s, pool_w), F32)] + [wide] * 2
        out_specs = [row(d)] * 3 + [row(pool_w)] + [row(d)] * 2
        tile_bytes = _nbytes((row_tile, d), F32) * 6 + _nbytes((row_tile, pool_w), F32)
        resident = _nbytes((d, in_w), BF16)
        grid_spec = pltpu.PrefetchScalarGridSpec(
            num_scalar_prefetch=0, grid=(nseq, tiles), in_specs=in_specs, out_specs=out_specs)
    return pl.pallas_call(
        functools.partial(_project_in_kernel, d_model=d, prompt_extras=prompt_extras, scale=scale,
                          rider_args=rider_args),
        grid_spec=grid_spec,
        out_shape=out_shape,
        compiler_params=pltpu.CompilerParams(
            dimension_semantics=("arbitrary", "arbitrary"),
            vmem_limit_bytes=_vmem_limit(tile_bytes, resident, 6 * _nbytes((row_tile, d), F32))),
    )(*operands)


def _moba_prompt_kernel(pt_ref, qs_ref, bias_ref, kb_ref, vt_ref, vt_own_ref, ck_ref, o_ref, ksum_ref,
                        kaug_ref, s_ref, pbuf, psem, *, rider_args):
    hg = pl.program_id(1)
    i = pl.program_id(2)
    seq = kb_ref.shape[0]
    n_heads = kaug_ref.shape[0]
    dh = kb_ref.shape[1] // n_heads
    n_blocks = seq // MOBA_BLOCK
    pair = 2 * MOBA_BLOCK

    n_steps = pl.num_programs(0) * pl.num_programs(1) * pl.num_programs(2)
    step = (pl.program_id(0) * pl.num_programs(1) + hg) * pl.num_programs(2) + i
    rider = _PageRider(pt_ref, ck_ref, pbuf, psem, ksum_ref, step=step, n_steps=n_steps, **rider_args)
    rider.start()

    @pl.when(i == 0)
    def _():
        row_blk = lax.broadcasted_iota(jnp.int32, (seq, V7X_LANES), 0) // MOBA_BLOCK
        lane = lax.broadcasted_iota(jnp.int32, (seq, V7X_LANES), 1)
        for hh in range(n_heads):
            head = hg * n_heads + hh
            kaug_ref[hh, :, :dh] = kb_ref[:, hh * dh:(hh + 1) * dh]
            kaug_ref[hh, :, dh:] = jnp.where(lane == head * HEAD_BIAS_LANES + row_blk,
                                             1.0, 0.0).astype(BF16)

    own = pl.multiple_of(i * MOBA_BLOCK, MOBA_BLOCK)
    bias = bias_ref[...]
    heads = range(n_heads)
    cols = [slice(hh * dh, (hh + 1) * dh) for hh in heads]
    qs = [qs_ref[:, cols[hh]] for hh in heads]
    q_aug = [jnp.concatenate([qs[hh], bias], axis=1) for hh in heads]

    def attend(n_past):
        chunks = [slice(c * pair, (c + 1) * pair) for c in range(n_past // 2)]
        own_rows = slice(n_past * MOBA_BLOCK, (n_past + 1) * MOBA_BLOCK)
        m = []
        for hh in heads:
            s = _dot_nt(kb_ref[pl.ds(own, MOBA_BLOCK), cols[hh]], qs[hh])
            key_i = lax.broadcasted_iota(jnp.int32, s.shape, 0)
            qry_i = lax.broadcasted_iota(jnp.int32, s.shape, 1)
            s = jnp.where(key_i <= qry_i, s, NEG_INF)
            s_ref[hh, own_rows, :] = s
            m.append(jnp.max(s, axis=0, keepdims=True))
        for rows in chunks:
            for hh in heads:
                s = _dot_nt(kaug_ref[hh, rows, :], q_aug[hh])
                s_ref[hh, rows, :] = s
                m[hh] = jnp.maximum(m[hh], jnp.max(s, axis=0, keepdims=True))
        l, acc = [], []
        for hh in heads:
            p = jnp.exp2(s_ref[hh, own_rows, :] - m[hh])
            l.append(jnp.sum(p, axis=0, keepdims=True))
            acc.append(jnp.dot(vt_own_ref[cols[hh], :], p.astype(BF16),
                               preferred_element_type=F32))
        for rows in chunks:
            for hh in heads:
                p = jnp.exp2(s_ref[hh, rows, :] - m[hh])
                l[hh] = l[hh] + jnp.sum(p, axis=0, keepdims=True)
                acc[hh] = acc[hh] + jnp.dot(vt_ref[cols[hh], rows], p.astype(BF16),
                                            preferred_element_type=F32)
        for hh in heads:
            o_ref[:, cols[hh]] = (acc[hh] / l[hh]).T

    n_quads = (i + MOBA_QUAD - 1) // MOBA_QUAD
    for quads in range((n_blocks - 1 + MOBA_QUAD - 1) // MOBA_QUAD + 1):
        pl.when(n_quads == quads)(functools.partial(attend, min(quads * MOBA_QUAD, n_blocks)))

    rider.drain()


def _moba_prompt(qs, bias, kb, vt, page_list, cache_k, first_page, n_pages):
    b, s, d = qs.shape
    dh = d // N_HEADS
    assert s % (2 * MOBA_BLOCK) == 0 and dh == V7X_LANES
    n_blocks = s // MOBA_BLOCK
    assert n_blocks <= HEAD_BIAS_LANES
    tq = MOBA_BLOCK
    hps = MOBA_HEADS_PER_STEP
    hw = hps * dh
    head_groups = N_HEADS // hps
    rider_args, rider_scratch, ksum_block, ksum_out = _rider_plan(
        first_page, n_pages, b * head_groups * n_blocks, cache_k.shape[1:])
    tile = pl.BlockSpec((None, tq, hw), lambda bi, h, i, pt: (bi, i, h))
    grid_spec = pltpu.PrefetchScalarGridSpec(
        num_scalar_prefetch=1,
        grid=(b, head_groups, n_blocks),
        in_specs=[tile,
                  pl.BlockSpec((None, tq, V7X_LANES), lambda bi, h, i, pt: (bi, i, 0)),
                  pl.BlockSpec((None, s, hw), lambda bi, h, i, pt: (bi, 0, h)),
                  pl.BlockSpec((None, hw, s), lambda bi, h, i, pt: (bi, h, 0)),
                  pl.BlockSpec((None, hw, tq), lambda bi, h, i, pt: (bi, h, i)),
                  pl.BlockSpec(memory_space=pl.ANY)],
        out_specs=[tile,
                   pl.BlockSpec(ksum_block,
                                lambda bi, h, i, pt: ((bi * head_groups + h) * n_blocks + i, 0, 0))],
        scratch_shapes=[pltpu.VMEM((hps, s, dh + V7X_LANES), BF16),
                        pltpu.VMEM((hps, s + MOBA_BLOCK, tq), F32)] + rider_scratch)
    return pl.pallas_call(
        functools.partial(_moba_prompt_kernel, rider_args=rider_args),
        grid_spec=grid_spec,
        out_shape=[jax.ShapeDtypeStruct((b, s, d), F32), ksum_out],
        compiler_params=pltpu.CompilerParams(
            dimension_semantics=("arbitrary", "arbitrary", "arbitrary"),
            vmem_limit_bytes=_vmem_limit(
                2 * _nbytes((s, hw), BF16) + 3 * _nbytes((tq, hw), F32) + _nbytes(ksum_block, F32),
                _nbytes((hps, s, dh + V7X_LANES), BF16) + _nbytes((hps, s + MOBA_BLOCK, tq), F32)
                + _nbytes(rider_scratch[0].shape, F32),
                16 * hps * _nbytes((2 * MOBA_BLOCK, tq), F32))),
    )(page_list, qs, bias, kb, vt, vt, cache_k)


def _sample_select_kernel(q_ref, ksum_ref, idx_ref):
    n_seqs, n_blocks, n_heads, dh = ksum_ref.shape
    t_new = q_ref.shape[1]
    blk = lax.broadcasted_iota(jnp.int32, (n_blocks, n_heads, 1), 0).astype(F32)
    lane = lax.broadcasted_iota(jnp.int32, (n_heads, V7X_LANES), 1)
    for q in range(n_seqs):
        kmean = ksum_ref[q] * (1.0 / MOBA_BLOCK)
        for t in range(t_new):
            g = jnp.sum(kmean * q_ref[q, t][None], axis=-1, keepdims=True)
            tile = jnp.zeros((n_heads, V7X_LANES), F32)
            for r in range(MOBA_TOP_K):
                m = jnp.max(g, axis=0, keepdims=True)
                first = jnp.min(jnp.where(g == m, blk, float(n_blocks)), axis=0, keepdims=True)
                tile = jnp.where(lane == r, first[0], tile)
                g = jnp.where(blk == first, PICKED, g)
            idx_ref[q, t] = tile.astype(jnp.int32)


def _sample_select(q4, ksum4):
    db, t_new, n_heads, dh = q4.shape
    n_blocks = ksum4.shape[1]
    assert n_blocks >= MOBA_TOP_K and n_heads == V7X_SUBLANES and dh == V7X_LANES
    per_step = SELECT_SEQS_PER_STEP if db % SELECT_SEQS_PER_STEP == 0 else 1
    return pl.pallas_call(
        _sample_select_kernel,
        grid=(db // per_step,),
        in_specs=[pl.BlockSpec((per_step, t_new, n_heads, dh), lambda b: (b, 0, 0, 0)),
                  pl.BlockSpec((per_step, n_blocks, n_heads, dh), lambda b: (b, 0, 0, 0))],
        out_specs=pl.BlockSpec((per_step, t_new, n_heads, V7X_LANES), lambda b: (b, 0, 0, 0)),
        out_shape=jax.ShapeDtypeStruct((db, t_new, n_heads, V7X_LANES), jnp.int32),
        compiler_params=pltpu.CompilerParams(
            dimension_semantics=("arbitrary",),
            vmem_limit_bytes=_vmem_limit(
                per_step * (_nbytes((n_blocks, n_heads, dh), F32)
                            + 2 * _nbytes((t_new, n_heads, V7X_LANES), F32)),
                0, 8 * _nbytes((n_blocks, n_heads, dh), F32))),
    )(q4, ksum4)


def _sample_attend_kernel(pt_ref, sel_ref, q_ref, kn_ref, vn_ref, ck_ref, cv_ref, o_ref,
                          kg, vg, sem, *, pages_per_block, scale):
    b = pl.program_id(0)
    n_seqs = pl.num_programs(0)
    t_new, d = q_ref.shape
    dh = d // N_HEADS
    rows_per_token = MOBA_TOP_K * MOBA_BLOCK

    def unit_copies(seq, h):
        slot = h % GATHER_SLOTS
        copies = []
        for t in range(t_new):
            for r in range(MOBA_TOP_K):
                blk = sel_ref[seq, (t * N_HEADS + h) * MOBA_TOP_K + r]
                for pp in range(pages_per_block):
                    phys = pt_ref[seq, blk * pages_per_block + pp]
                    row0 = ((t * MOBA_TOP_K + r) * pages_per_block + pp) * PAGE_SIZE
                    dst = pl.ds(row0, PAGE_SIZE)
                    copies.append(pltpu.make_async_copy(
                        ck_ref.at[phys, :, h, :], kg.at[slot, dst, :], sem.at[0, slot]))
                    copies.append(pltpu.make_async_copy(
                        cv_ref.at[phys, :, h, :], vg.at[slot, dst, :], sem.at[1, slot]))
        return copies

    def start_unit(seq, h):
        for c in unit_copies(seq, h):
            c.start()

    @pl.when(b == 0)
    def _():
        for h in range(GATHER_AHEAD):
            start_unit(0, h)

    q = q_ref[...]
    kn = kn_ref[...]
    vn = vn_ref[...]
    own_row = lax.broadcasted_iota(jnp.int32, (t_new, 1), 0)

    for h in range(N_HEADS):
        ahead = h + GATHER_AHEAD
        if ahead < N_HEADS:
            start_unit(b, ahead)
        else:
            pl.when(b + 1 < n_seqs)(functools.partial(start_unit, b + 1, ahead - N_HEADS))
        for c in unit_copies(b, h):
            c.wait()
        slot = h % GATHER_SLOTS
        sl = slice(h * dh, (h + 1) * dh)
        kn_h = kn[:, sl]
        vn_h = vn[:, sl]
        for t in range(t_new):
            rows = pl.ds(t * rows_per_token, rows_per_token)
            qrow = q[t:t + 1, sl] * (scale * LOG2_E)
            s = jnp.sum(kg[slot, rows, :] * qrow, axis=-1, keepdims=True)
            s_own = jnp.sum(kn_h * qrow, axis=-1, keepdims=True)
            s_own = jnp.where(own_row <= t, s_own, NEG_INF)
            m = jnp.maximum(jnp.max(s, axis=0, keepdims=True), jnp.max(s_own, axis=0, keepdims=True))
            p = jnp.exp2(s - m)
            p_own = jnp.exp2(s_own - m)
            denom = jnp.sum(p, axis=0, keepdims=True) + jnp.sum(p_own, axis=0, keepdims=True)
            o = (jnp.sum(p * vg[slot, rows, :], axis=0, keepdims=True)
                 + jnp.sum(p_own * vn_h, axis=0, keepdims=True))
            o_ref[t:t + 1, sl] = o / denom


def _sample_attend(page_table, sel, q3, kn3, vn3, cache_k, cache_v, *, scale):
    db, t_new, d = q3.shape
    dh = d // N_HEADS
    pages_per_block = MOBA_BLOCK // PAGE_SIZE
    assert N_HEADS % GATHER_SLOTS == 0 and GATHER_AHEAD < GATHER_SLOTS
    gathered = (GATHER_SLOTS, t_new * MOBA_TOP_K * MOBA_BLOCK, dh)
    tok = pl.BlockSpec((None, t_new, d), lambda b, pt, sel: (b, 0, 0))
    grid_spec = pltpu.PrefetchScalarGridSpec(
        num_scalar_prefetch=2,
        grid=(db,),
        in_specs=[tok, tok, tok, pl.BlockSpec(memory_space=pl.ANY), pl.BlockSpec(memory_space=pl.ANY)],
        out_specs=tok,
        scratch_shapes=[pltpu.VMEM(gathered, F32), pltpu.VMEM(gathered, F32),
                        pltpu.SemaphoreType.DMA((2, GATHER_SLOTS))])
    return pl.pallas_call(
        functools.partial(_sample_attend_kernel, pages_per_block=pages_per_block, scale=scale),
        grid_spec=grid_spec,
        out_shape=jax.ShapeDtypeStruct((db, t_new, d), F32),
        compiler_params=pltpu.CompilerParams(
            dimension_semantics=("arbitrary",),
            vmem_limit_bytes=_vmem_limit(4 * _nbytes((V7X_SUBLANES, d), F32),
                                         2 * _nbytes(gathered, F32),
                                         8 * _nbytes((MOBA_TOP_K * MOBA_BLOCK, dh), F32))),
    )(page_table, sel, q3, kn3, vn3, cache_k, cache_v)


def _pool_mix_tile(u_ref, halo_ref, gp_ref, ga_ref, attn_ref, wp_ref, ps_ref, uext, pos0):
    t_rows, pool_w = u_ref.shape
    t_pad = uext.shape[0] - POOL_HALO
    gdim = pool_w // N_POOL_GROUPS
    gout = wp_ref.shape[-1]

    halo_pos = pos0 - POOL_HALO + lax.broadcasted_iota(jnp.int32, (POOL_HALO, 1), 0)
    uext[0:POOL_HALO, :] = jnp.where(halo_pos >= 0, halo_ref[...], 0.0)
    uext[POOL_HALO:POOL_HALO + t_rows, :] = u_ref[...]
    if t_pad > t_rows:
        uext[POOL_HALO + t_rows:, :] = jnp.zeros((t_pad - t_rows, pool_w), F32)

    pos = pos0 + lax.broadcasted_iota(jnp.int32, (t_pad, 1), 0)
    slabs = []
    for g, window in enumerate(POOL_WINDOWS):
        cols = slice(g * gdim, (g + 1) * gdim)
        tok = uext[POOL_HALO:POOL_HALO + t_pad, cols]
        total = tok
        for r in range(1, window):
            total = total + uext[POOL_HALO - r:POOL_HALO - r + t_pad, cols]
        cnt = jnp.minimum(window, pos + 1).astype(F32)
        diff = total / cnt - tok
        y = jnp.dot(diff.astype(BF16), wp_ref[g], preferred_element_type=F32)[:t_rows]
        oc = slice(g * gout, (g + 1) * gout)
        pool = y * ps_ref[:, oc]
        slabs.append(jax.nn.sigmoid(gp_ref[:, oc]) * pool
                     + jax.nn.sigmoid(ga_ref[:, oc]) * attn_ref[:, oc])
    return slabs


def _mlp_tile(x, mixed, wo_ref, g2_ref, wu_ref, wd_ref):
    d_ff = wu_ref.shape[1]
    hres = x + jnp.dot(mixed.astype(BF16), wo_ref[...], preferred_element_type=F32)
    hn = (hres * _rms_scale(hres) * g2_ref[...]).astype(BF16)
    acc = hres
    for c in range(d_ff // FF_CHUNK):
        cols = slice(c * FF_CHUNK, (c + 1) * FF_CHUNK)
        a = jnp.dot(hn, wu_ref[:, cols], preferred_element_type=F32)
        a = jnp.square(jnp.maximum(a, 0.0)).astype(BF16)
        acc = acc + jnp.dot(a, wd_ref[cols, :], preferred_element_type=F32)
    return acc


def _pool_mix_kernel(u_ref, halo_ref, gp_ref, ga_ref, attn_ref, wp_ref, ps_ref, o_ref, uext, *,
                     base_pos):
    pos0 = base_pos + pl.program_id(1) * u_ref.shape[0]
    slabs = _pool_mix_tile(u_ref, halo_ref, gp_ref, ga_ref, attn_ref, wp_ref, ps_ref, uext, pos0)
    gout = wp_ref.shape[-1]
    for g, slab in enumerate(slabs):
        o_ref[:, g * gout:(g + 1) * gout] = slab


def _out_mlp_kernel(x_ref, mixed_ref, wo_ref, g2_ref, wu_ref, wd_ref, y_ref):
    y_ref[...] = _mlp_tile(x_ref[...], mixed_ref[...], wo_ref, g2_ref, wu_ref, wd_ref)


def _mix_mlp_kernel(pt_ref, x_ref, u_ref, halo_ref, gp_ref, ga_ref, attn_ref, wp_ref, ps_ref,
                    wo_ref, g2_ref, wu_ref, wd_ref, ck_ref, y_ref, ksum_ref, uext, pbuf, psem, *,
                    base_pos, rider_args):
    step = pl.program_id(0) * pl.num_programs(1) + pl.program_id(1)
    rider = _PageRider(pt_ref, ck_ref, pbuf, psem, ksum_ref, step=step,
                       n_steps=pl.num_programs(0) * pl.num_programs(1), **rider_args)
    rider.start()
    rider.drain()
    pos0 = base_pos + pl.program_id(1) * u_ref.shape[0]
    slabs = _pool_mix_tile(u_ref, halo_ref, gp_ref, ga_ref, attn_ref, wp_ref, ps_ref, uext, pos0)
    mixed = jnp.concatenate(slabs, axis=1)
    y_ref[...] = _mlp_tile(x_ref[...], mixed, wo_ref, g2_ref, wu_ref, wd_ref)


def _pool_specs(row_tile, pool_w, d, halo_index, w_pool_shape):
    at = lambda f: (lambda q, i, *prefetch: f(q, i))
    wide = pl.BlockSpec((None, row_tile, d), at(lambda q, i: (q, i, 0)))
    return [pl.BlockSpec((None, row_tile, pool_w), at(lambda q, i: (q, i, 0))),
            pl.BlockSpec((None, POOL_HALO, pool_w), at(halo_index)),
            wide, wide, wide,
            pl.BlockSpec(w_pool_shape, at(lambda q, i: (0, 0, 0))),
            pl.BlockSpec((1, d), at(lambda q, i: (0, 0)))]


def _pool_mix(u3, halo3, halo_index, gp3, ga3, attn3, w_pool_bf16, pool_scale, *, row_tile, base_pos):
    nseq, s, pool_w = u3.shape
    d = gp3.shape[-1]
    assert s % row_tile == 0
    t_pad = -(-row_tile // V7X_SUBLANES) * V7X_SUBLANES
    return pl.pallas_call(
        functools.partial(_pool_mix_kernel, base_pos=base_pos),
        grid=(nseq, s // row_tile),
        in_specs=_pool_specs(row_tile, pool_w, d, halo_index, w_pool_bf16.shape),
        out_specs=pl.BlockSpec((None, row_tile, d), lambda q, i: (q, i, 0)),
        out_shape=jax.ShapeDtypeStruct((nseq, s, d), F32),
        scratch_shapes=[pltpu.VMEM((POOL_HALO + t_pad, pool_w), F32)],
        compiler_params=pltpu.CompilerParams(
            dimension_semantics=("arbitrary", "arbitrary"),
            vmem_limit_bytes=_vmem_limit(5 * _nbytes((t_pad, d), F32),
                                         _nbytes(w_pool_bf16.shape, BF16),
                                         6 * _nbytes((t_pad, d), F32))),
    )(u3, halo3, gp3, ga3, attn3, w_pool_bf16, pool_scale)


def _mlp_weight_specs(d, d_ff, index):
    return [_resident((d, d), index), pl.BlockSpec((1, d), index),
            _resident((d, d_ff), index), _resident((d_ff, d), index)]


def _out_mlp(x2d, mixed2d, w_out, g2, w_up, w_down, *, row_tile):
    n, d = x2d.shape
    d_ff = w_up.shape[1]
    assert n % row_tile == 0 and d_ff % FF_CHUNK == 0
    row = pl.BlockSpec((row_tile, d), lambda i: (i, 0))
    weights = _nbytes((d, d), BF16) + 2 * _nbytes((d, d_ff), BF16)
    return pl.pallas_call(
        _out_mlp_kernel,
        grid=(n // row_tile,),
        in_specs=[row, row] + _mlp_weight_specs(d, d_ff, lambda i: (0, 0)),
        out_specs=row,
        out_shape=jax.ShapeDtypeStruct((n, d), F32),
        compiler_params=pltpu.CompilerParams(
            dimension_semantics=("arbitrary",),
            vmem_limit_bytes=_vmem_limit(3 * _nbytes((row_tile, d), F32), weights,
                                         6 * _nbytes((row_tile, FF_CHUNK), F32))),
    )(x2d, mixed2d, w_out, g2, w_up, w_down)


def _mix_mlp(x3, u3, halo3, halo_index, gp3, ga3, attn3, w_pool_bf16, pool_scale,
             w_out, g2, w_up, w_down, page_list, cache_k, first_page, n_pages, *, row_tile, base_pos):
    nseq, s, d = x3.shape
    pool_w = u3.shape[-1]
    d_ff = w_up.shape[1]
    assert s % row_tile == 0 and row_tile % V7X_SUBLANES == 0 and d_ff % FF_CHUNK == 0
    tiles = s // row_tile
    rider_args, rider_scratch, ksum_block, ksum_out = _rider_plan(
        first_page, n_pages, nseq * tiles, cache_k.shape[1:])
    wide = pl.BlockSpec((None, row_tile, d), lambda q, i, pt: (q, i, 0))
    weights = _nbytes((d, d), BF16) + 2 * _nbytes((d, d_ff), BF16) + _nbytes(w_pool_bf16.shape, BF16)
    grid_spec = pltpu.PrefetchScalarGridSpec(
        num_scalar_prefetch=1,
        grid=(nseq, tiles),
        in_specs=([wide] + _pool_specs(row_tile, pool_w, d, halo_index, w_pool_bf16.shape)
                  + _mlp_weight_specs(d, d_ff, lambda q, i, pt: (0, 0))
                  + [pl.BlockSpec(memory_space=pl.ANY)]),
        out_specs=[wide, pl.BlockSpec(ksum_block, lambda q, i, pt: (q * tiles + i, 0, 0))],
        scratch_shapes=[pltpu.VMEM((POOL_HALO + row_tile, pool_w), F32)] + rider_scratch)
    return pl.pallas_call(
        functools.partial(_mix_mlp_kernel, base_pos=base_pos, rider_args=rider_args),
        grid_spec=grid_spec,
        out_shape=[jax.ShapeDtypeStruct((nseq, s, d), F32), ksum_out],
        compiler_params=pltpu.CompilerParams(
            dimension_semantics=("arbitrary", "arbitrary"),
            vmem_limit_bytes=_vmem_limit(
                5 * _nbytes((row_tile, d), F32) + _nbytes((row_tile, pool_w), F32)
                + _nbytes(ksum_block, F32),
                weights + _nbytes(rider_scratch[0].shape, F32),
                8 * _nbytes((row_tile, FF_CHUNK), F32))),
    )(page_list, x3, u3, halo3, gp3, ga3, attn3, w_pool_bf16, pool_scale, w_out, g2, w_up, w_down,
      cache_k)


def kernel(x_prompt, x_sample, cache_k, cache_v, state_pool, page_table, norm1_g, w_in,
           q_norm_g, k_norm_g, w_pool, pool_scale, w_out, norm2_g, w_up, w_down):
    depth = w_in.shape[0]
    b, s, d = x_prompt.shape
    db, t_new, _ = x_sample.shape
    dh = d // N_HEADS
    n_pages = page_table.shape[1]
    past = n_pages * PAGE_SIZE
    scale = dh ** -0.5
    pool_buf = state_pool.shape[2]
    assert pool_buf == POOL_HALO - 1
    assert n_pages % (MOBA_BLOCK // PAGE_SIZE) == 0, "cached rows must fill whole MoBA blocks"
    n_cached = n_pages // (MOBA_BLOCK // PAGE_SIZE)

    xp = x_prompt
    xs = x_sample.reshape(db * t_new, d)
    outs = [[] for _ in range(6)]
    for l in range(depth):
        w_in_l = w_in[l].astype(BF16)
        w_pool_l = w_pool[l].astype(BF16)
        w_out_l = w_out[l].astype(BF16)
        w_up_l = w_up[l].astype(BF16)
        w_down_l = w_down[l].astype(BF16)
        g1, g2 = norm1_g[l][None], norm2_g[l][None]
        qg, kg = q_norm_g[l][None], k_norm_g[l][None]
        ps = pool_scale[l][None]

        page_list = page_table.reshape(-1)
        pages_per_block = MOBA_BLOCK // PAGE_SIZE
        total_blocks = db * n_pages // pages_per_block
        n_in_pages = total_blocks * RIDER_SHARE_PROJECT_IN // RIDER_SHARE_TOTAL * pages_per_block
        n_mlp_pages = total_blocks * RIDER_SHARE_MIX_MLP // RIDER_SHARE_TOTAL * pages_per_block
        n_moba_pages = db * n_pages - n_in_pages - n_mlp_pages
        k, v, u, gp, ga, qs, bias, kb, vt, ksum_a = _project_in(
            xp, g1, w_in_l, qg, kg, row_tile=MOBA_BLOCK, scale=scale,
            rider=(page_list, cache_k[l], 0, n_in_pages))
        attn, ksum_b = _moba_prompt(qs, bias, kb, vt, page_list, cache_k[l], n_in_pages, n_moba_pages)
        tile = MOBA_BLOCK
        halo_blocks = tile // POOL_HALO
        xp, ksum_c = _mix_mlp(
            xp, u, u, lambda qi, i: (qi, jnp.maximum(i * halo_blocks - 1, 0), 0), gp, ga, attn,
            w_pool_l, ps, w_out_l, g2, w_up_l, w_down_l, page_list, cache_k[l],
            n_in_pages + n_moba_pages, n_mlp_pages, row_tile=tile, base_pos=0)
        cache_ksum = jnp.concatenate([ksum_a[:n_in_pages // pages_per_block],
                                      ksum_b[:n_moba_pages // pages_per_block],
                                      ksum_c[:n_mlp_pages // pages_per_block]])
        outs[0].append(k.reshape(b, s, N_HEADS, dh))
        outs[1].append(v.reshape(b, s, N_HEADS, dh))
        outs[2].append(u[:, s - pool_buf:])

        q, k, v, u, gp, ga = _project_in(xs[None], g1, w_in_l, qg, kg, row_tile=db * t_new, scale=scale)
        tok3 = lambda a: a.reshape(db, t_new, a.shape[-1])
        picked = _sample_select(q.reshape(db, t_new, N_HEADS, dh),
                                cache_ksum.reshape(db, n_cached, N_HEADS, dh))
        sel = picked[:, :, :, :MOBA_TOP_K].reshape(db, t_new * N_HEADS * MOBA_TOP_K)
        attn = _sample_attend(page_table, sel, tok3(q), tok3(k), tok3(v), cache_k[l], cache_v[l],
                              scale=scale)
        state = state_pool[l].astype(F32)
        halo = jnp.pad(state, ((0, 0), (POOL_HALO - pool_buf, 0), (0, 0)))
        mixed = _pool_mix(
            tok3(u), halo, lambda qi, i: (qi, 0, 0), tok3(gp), tok3(ga), attn, w_pool_l, ps,
            row_tile=t_new, base_pos=past)
        xs = _out_mlp(xs, mixed.reshape(db * t_new, d), w_out_l, g2, w_up_l, w_down_l,
                      row_tile=db * t_new)
        outs[3].append(k.reshape(db, t_new, N_HEADS, dh))
        outs[4].append(v.reshape(db, t_new, N_HEADS, dh))
        outs[5].append(jnp.concatenate([state, tok3(u)], axis=1)[:, -pool_buf:])

    stacked = [jnp.stack(o, axis=0) for o in outs]
    return (xp, xs.reshape(db, t_new, d), *stacked)
```
